```python
import math
import jax
import jax.numpy as jnp
from jax import lax
import numpy as np


D_MODEL = 1024
BATCH = 8
SEQ = 2048
DEPTH = 2

GRID_W = 64
CTX_LEN = 256
N_MIXERS = 2
EPS = 1e-6
DIFF_HEADS = 8
DIFF_HEAD_DIM = D_MODEL // (2 * DIFF_HEADS)
ROPE_THETA = 10000.0
ROPE_AXIS_DIM = DIFF_HEAD_DIM // 2
ROPE_FREQS = ROPE_AXIS_DIM // 2
Q_BLOCK = 128
D_RNN = D_MODEL
LRU_BLOCKS = 8
LRU_BW = D_RNN // LRU_BLOCKS
CONV_W = 4
LRU_C = 8.0
N_EXPERTS = 16
EC_CAPACITY = 2
D_EXPERT = 2816
N_ATTN = (DEPTH + 1) // 2
N_LRU = DEPTH // 2

kernel_name = 'hybrid_diffattn_rglru_ecmoe_dit'

F32 = jnp.float32


def rmsnorm(x, g):
    xf = x.astype(F32)
    y = xf * lax.rsqrt(jnp.mean(xf * xf, axis=-1, keepdims=True) + EPS)
    return (y * g.astype(F32)).astype(x.dtype)


def modulate(x, shift, scale):
    return x * (1.0 + scale) + shift


def axial_rope_tables(n):
    rows = n // GRID_W
    row = jnp.repeat(jnp.arange(rows), GRID_W).astype(F32)
    col = jnp.tile(jnp.arange(GRID_W), rows).astype(F32)
    inv = ROPE_THETA ** (-(jnp.arange(ROPE_FREQS, dtype=F32) * 2.0) / ROPE_AXIS_DIM)
    ang = jnp.concatenate([row[:, None] * inv, col[:, None] * inv], axis=-1)
    return jnp.cos(ang), jnp.sin(ang)


def apply_axial_rope(x, cos, sin):
    n = x.shape[1]
    xr = x.astype(F32).reshape(*x.shape[:-1], 2, 2, ROPE_FREQS)
    x1, x2 = xr[..., 0, :], xr[..., 1, :]
    c = cos.reshape(n, 1, 2, ROPE_FREQS)
    s = sin.reshape(n, 1, 2, ROPE_FREQS)
    out = jnp.stack([x1 * c - x2 * s, x2 * c + x1 * s], axis=-2)
    return out.reshape(x.shape).astype(x.dtype)


def diff_attn_core(q, k, v, lam):
    s = jnp.einsum('bqhd,bkhd->bhqk', q.astype(F32), k.astype(F32))
    p = jax.nn.softmax(s, axis=-1)
    b_, _, lq, lk = p.shape
    p = p.reshape(b_, DIFF_HEADS, 2, lq, lk)
    w = p[:, :, 0] - lam * p[:, :, 1]
    return jnp.einsum('bhqk,bkhe->bqhe', w, v.astype(F32))


def diff_attention(n_lat, n_ctx, w_qkv, lq1, lk1, lq2, lk2, subln_g, w_o, lambda_init, ctx_out):
    b_, n, _ = n_lat.shape
    H, dh = DIFF_HEADS, DIFF_HEAD_DIM
    lam = (jnp.exp(jnp.sum(lq1.astype(F32) * lk1.astype(F32)))
           - jnp.exp(jnp.sum(lq2.astype(F32) * lk2.astype(F32))) + lambda_init)

    def project(h):
        L = h.shape[1]
        q, k, v = jnp.split(h @ w_qkv, 3, axis=-1)
        return (q.reshape(b_, L, 2 * H, dh) * (dh ** -0.5),
                k.reshape(b_, L, 2 * H, dh),
                v.reshape(b_, L, H, 2 * dh))

    q_l, k_l, v_l = project(n_lat)
    q_c, k_c, v_c = project(n_ctx)
    cos, sin = axial_rope_tables(n)
    q_l = apply_axial_rope(q_l, cos, sin)
    k_l = apply_axial_rope(k_l, cos, sin)
    k_all = jnp.concatenate([k_l, k_c], axis=1)
    v_all = jnp.concatenate([v_l, v_c], axis=1)
    nb = n // Q_BLOCK
    q_blocks = q_l.reshape(b_, nb, Q_BLOCK, 2 * H, dh).swapaxes(0, 1)
    o_blocks = lax.map(lambda qb: diff_attn_core(qb, k_all, v_all, lam), q_blocks)
    o_l = o_blocks.swapaxes(0, 1).reshape(b_, n, H, 2 * dh)

    def finish(o):
        o = rmsnorm(o, subln_g) * (1.0 - lambda_init)
        return o.reshape(b_, o.shape[1], D_MODEL).astype(n_lat.dtype) @ w_o

    y_l = finish(o_l)
    y_c = finish(diff_attn_core(q_c, k_c, v_c, lam)) if ctx_out else None
    return y_l, y_c


def centred_depthwise_conv(x, w, b):
    left = CONV_W // 2
    right = CONV_W - 1 - left
    y = lax.conv_general_dilated(x, w[:, None, :].astype(x.dtype), window_strides=(1,),
                                 padding=[(left, right)],
                                 dimension_numbers=('NWC', 'WIO', 'NWC'),
                                 feature_group_count=x.shape[-1])
    return y + b.astype(x.dtype)


def linear_scan(a, b, h0):
    def combine(l, r):
        return l[0] * r[0], r[0] * l[1] + r[1]
    A, Bc = lax.associative_scan(combine, (a, b), axis=1)
    return A * h0[:, None, :] + Bc


def rglru_block(n_lat, n_ctx, w_in, b_in, conv_w, conv_b, w_gates, b_gates, lam_p, w_out, ctx_out):
    def gate_terms(xc, d):
        b_, L, _ = xc.shape
        xf = xc.astype(F32)
        g = jnp.einsum('blkc,kcd->blkd', xf.reshape(b_, L, LRU_BLOCKS, LRU_BW),
                       w_gates[d].astype(F32)) + b_gates[d].astype(F32)
        r = jax.nn.sigmoid(g[..., :LRU_BW]).reshape(b_, L, D_RNN)
        i = jax.nn.sigmoid(g[..., LRU_BW:]).reshape(b_, L, D_RNN)
        log_a = -LRU_C * r * jax.nn.softplus(-lam_p[d].astype(F32))
        a = jnp.exp(log_a)
        bt = jnp.sqrt(-jnp.expm1(2.0 * log_a)) * (i * xf)
        return a, bt

    y_l, xr_l = jnp.split(n_lat @ w_in + b_in, 2, axis=-1)
    x_l = centred_depthwise_conv(xr_l, conv_w, conv_b)
    if ctx_out:
        y_c, xr_c = jnp.split(n_ctx @ w_in + b_in, 2, axis=-1)
    else:
        xr_c = n_ctx @ w_in[:, D_RNN:] + b_in[D_RNN:]
    x_c = centred_depthwise_conv(xr_c, conv_w, conv_b)

    h_l = 0.0
    h_c = 0.0
    for d in range(2):
        a_c, b_c = gate_terms(x_c, d)
        a_l, b_l = gate_terms(x_l, d)
        if d == 1:
            a_c, b_c, a_l, b_l = (jnp.flip(t, axis=1) for t in (a_c, b_c, a_l, b_l))
        s_c = linear_scan(a_c, b_c, jnp.zeros_like(b_c[:, 0]))
        s_l = linear_scan(a_l, b_l, s_c[:, -1])
        if d == 1:
            s_l = jnp.flip(s_l, axis=1)
            s_c = jnp.flip(s_c, axis=1)
        h_l = h_l + s_l
        if ctx_out:
            h_c = h_c + s_c
    out_l = (jax.nn.gelu(y_l.astype(F32)) * h_l).astype(n_lat.dtype) @ w_out
    out_c = ((jax.nn.gelu(y_c.astype(F32)) * h_c).astype(n_ctx.dtype) @ w_out) if ctx_out else None
    return out_l, out_c


def expert_choice_moe(h, w_router, w_gate_up, w_down):
    b_, L, _ = h.shape
    cap = EC_CAPACITY * L // N_EXPERTS
    logits = jnp.einsum('bld,de->bel', h.astype(F32), w_router.astype(F32))
    probs = jax.nn.softmax(logits, axis=1)
    gate, idx = lax.top_k(probs, cap)
    bidx = jnp.arange(b_)[:, None, None]
    xg = h[bidx, idx]
    gu = jnp.einsum('becd,edf->becf', xg, w_gate_up)
    g_, u_ = jnp.split(gu, 2, axis=-1)
    y = jnp.einsum('becf,efd->becd', jax.nn.silu(g_) * u_, w_down)
    y = (y.astype(F32) * gate[..., None]).astype(h.dtype)
    return jnp.zeros_like(h).at[bidx, idx].add(y)


def setup_inputs(seed: int = 0) -> dict:
    key = jax.random.key(seed)
    ks = jax.random.split(key, 32)
    D = D_MODEL

    def nrm(k, shape, fan_in):
        return jax.random.normal(k, shape, F32) * (fan_in ** -0.5)

    def small(k, shape, s=0.02):
        return jax.random.normal(k, shape, F32) * s

    u = jax.random.uniform(ks[20], (N_LRU, 2, D_RNN), F32, minval=0.9, maxval=0.999)
    base = u ** (1.0 / LRU_C)
    lru_lambda = jnp.log(base) - jnp.log1p(-base)

    return {
        'x': jax.random.normal(ks[0], (BATCH, SEQ, D), F32),
        'c': jax.random.normal(ks[1], (BATCH, D), F32),
        'ctx': jax.random.normal(ks[2], (BATCH, CTX_LEN, D), F32),
        'c_ctx': jax.random.normal(ks[3], (D,), F32),
        'ada_w': nrm(ks[4], (DEPTH, D, 6 * D), D),
        'ada_b': small(ks[5], (DEPTH, 6 * D)),
        'norm1_g': 1.0 + small(ks[6], (DEPTH, D)),
        'norm2_g': 1.0 + small(ks[7], (DEPTH, D)),
        'final_g': 1.0 + small(ks[8], (D,)),
        'attn_w_qkv': nrm(ks[9], (N_ATTN, D, 3 * D), D),
        'attn_lq1': small(ks[10], (N_ATTN, DIFF_HEAD_DIM), 0.1),
        'attn_lk1': small(ks[11], (N_ATTN, DIFF_HEAD_DIM), 0.1),
        'attn_lq2': small(ks[12], (N_ATTN, DIFF_HEAD_DIM), 0.1),
        'attn_lk2': small(ks[13], (N_ATTN, DIFF_HEAD_DIM), 0.1),
        'attn_subln_g': 1.0 + small(ks[14], (N_ATTN, 2 * DIFF_HEAD_DIM)),
        'attn_w_o': nrm(ks[15], (N_ATTN, D, D), D),
        'lru_w_in': nrm(ks[16], (N_LRU, D, 2 * D_RNN), D),
        'lru_b_in': small(ks[17], (N_LRU, 2 * D_RNN)),
        'lru_conv_w': nrm(ks[18], (N_LRU, CONV_W, D_RNN), CONV_W),
        'lru_conv_b': small(ks[19], (N_LRU, D_RNN)),
        'lru_w_gates': nrm(ks[21], (N_LRU, 2, LRU_BLOCKS, LRU_BW, 2 * LRU_BW), LRU_BW),
        'lru_b_gates': small(ks[22], (N_LRU, 2, LRU_BLOCKS, 2 * LRU_BW)),
        'lru_lambda': lru_lambda,
        'lru_w_out': nrm(ks[23], (N_LRU, D_RNN, D), D_RNN),
        'moe_w_router': nrm(ks[24], (DEPTH, D, N_EXPERTS), D),
        'moe_w_gate_up': nrm(ks[25], (DEPTH, N_EXPERTS, D, 2 * D_EXPERT), D),
        'moe_w_down': nrm(ks[26], (DEPTH, N_EXPERTS, D_EXPERT, D), D_EXPERT),
    }


def reference(x, c, ctx, c_ctx, ada_w, ada_b, norm1_g, norm2_g, final_g,
              attn_w_qkv, attn_lq1, attn_lk1, attn_lq2, attn_lk2, attn_subln_g, attn_w_o,
              lru_w_in, lru_b_in, lru_conv_w, lru_conv_b, lru_w_gates, lru_b_gates, lru_lambda, lru_w_out,
              moe_w_router, moe_w_gate_up, moe_w_down):
    h_lat = x
    h_ctx = ctx
    for i in range(DEPTH):
        last = i == DEPTH - 1
        mod_l = (jax.nn.silu(c) @ ada_w[i] + ada_b[i])[:, None, :]
        mod_c = jax.nn.silu(c_ctx) @ ada_w[i] + ada_b[i]
        sh1, sc1, g1, sh2, sc2, g2 = jnp.split(mod_l, 6, axis=-1)
        csh1, csc1, cg1, csh2, csc2, cg2 = jnp.split(mod_c, 6, axis=-1)

        n_l = modulate(rmsnorm(h_lat, norm1_g[i]), sh1, sc1)
        n_c = modulate(rmsnorm(h_ctx, norm1_g[i]), csh1, csc1)
        j = i // N_MIXERS
        if i % N_MIXERS == 0:
            lambda_init = 0.8 - 0.6 * math.exp(-0.3 * i)
            y_l, y_c = diff_attention(n_l, n_c, attn_w_qkv[j], attn_lq1[j], attn_lk1[j],
                                      attn_lq2[j], attn_lk2[j], attn_subln_g[j], attn_w_o[j],
                                      lambda_init, not last)
        else:
            y_l, y_c = rglru_block(n_l, n_c, lru_w_in[j], lru_b_in[j], lru_conv_w[j], lru_conv_b[j],
                                   lru_w_gates[j], lru_b_gates[j], lru_lambda[j], lru_w_out[j], not last)
        h_lat = h_lat + (g1 * y_l).astype(h_lat.dtype)
        if not last:
            h_ctx = h_ctx + (cg1 * y_c).astype(h_ctx.dtype)

        m_l = modulate(rmsnorm(h_lat, norm2_g[i]), sh2, sc2)
        h_lat = h_lat + (g2 * expert_choice_moe(m_l, moe_w_router[i], moe_w_gate_up[i],
                                                moe_w_down[i])).astype(h_lat.dtype)
        if not last:
            m_c = modulate(rmsnorm(h_ctx, norm2_g[i]), csh2, csc2)
            h_ctx = h_ctx + (cg2 * expert_choice_moe(m_c, moe_w_router[i], moe_w_gate_up[i],
                                                    moe_w_down[i])).astype(h_ctx.dtype)
    return rmsnorm(h_lat, final_g)
```

```python
import functools
import math

import jax
import jax.numpy as jnp
from jax import lax
from jax.experimental import pallas as pl
from jax.experimental.pallas import tpu as pltpu

F32 = jnp.float32
BF16 = jnp.bfloat16
I32 = jnp.int32

EPS = 1e-6
GRID_W = 64
ROPE_THETA = 10000.0
LRU_C = 8.0
EC_CAPACITY = 2
N_MIXERS = 2
HEAD_W = 128
LRU_BW = 128
CONV_W = 4
MOD_ROWS = 16
LANES = 128
SUBLANES = 8
VMEM_LIMIT = 56 * 2**20


def _pick(n, cands):
    return next(c for c in cands if n % c == 0)


def _cparams(sem):
    return pltpu.CompilerParams(dimension_semantics=sem, vmem_limit_bytes=VMEM_LIMIT)


def _sigmoid(x):
    return 1.0 / (1.0 + jnp.exp(-x))


def _gelu_tanh(x):
    return 0.5 * x * (1.0 + jnp.tanh(math.sqrt(2.0 / math.pi) * (x + 0.044715 * (x * x * x))))


def _row_mod(mod_ref, b, ctx_row, row0, nrows, n_lat):
    mb = mod_ref[pl.ds(b, 1), :]
    mc = mod_ref[ctx_row:ctx_row + 1, :]
    rows = row0 + lax.broadcasted_iota(I32, (nrows, 1), 0)
    return jnp.where(rows < n_lat, mb, mc)


def _ln_mod(x, g, shift, scale):
    ms = jnp.mean(x * x, axis=-1, keepdims=True)
    y = (x * lax.rsqrt(ms + EPS)) * g
    return y * (1.0 + scale) + shift


def _mod_kernel(c_ref, w_ref, b_ref, o_ref):
    c = c_ref[...]
    s = c * _sigmoid(c)
    o_ref[...] = jnp.dot(s, w_ref[...], preferred_element_type=F32,
                         precision=lax.Precision.HIGHEST) + b_ref[...]


def _mod_tables(cc, ada_w, ada_b):
    depth, d, n6 = ada_w.shape
    tn = _pick(n6, (1536, 768, 384))
    return pl.pallas_call(
        _mod_kernel,
        grid=(depth, n6 // tn),
        in_specs=[pl.BlockSpec((MOD_ROWS, d), lambda i, j: (0, 0)),
                  pl.BlockSpec((None, d, tn), lambda i, j: (i, 0, j)),
                  pl.BlockSpec((None, 1, tn), lambda i, j: (i, 0, j))],
        out_specs=pl.BlockSpec((None, MOD_ROWS, tn), lambda i, j: (i, 0, j)),
        out_shape=jax.ShapeDtypeStruct((depth, MOD_ROWS, n6), F32),
        compiler_params=_cparams(("arbitrary", "arbitrary")),
        name="mod_tables",
    )(cc, ada_w, ada_b.reshape(depth, 1, n6))


def _qkv_kernel(h_ref, g_ref, sh_ref, sc_ref, wqk_ref, wvt_ref, cos_ref, s1_ref, s2_ref,
                qk_ref, vt_ref, *, n_lat, ctx_row):
    b = pl.program_id(0)
    i = pl.program_id(1)
    tt, d = h_ref.shape
    shift = _row_mod(sh_ref, b, ctx_row, i * tt, tt, n_lat)
    scale = _row_mod(sc_ref, b, ctx_row, i * tt, tt, n_lat)
    xn = _ln_mod(h_ref[...], g_ref[...], shift, scale).astype(BF16)
    cw = 2 * HEAD_W
    cos = jnp.concatenate([cos_ref[...]] * 2, axis=1)
    s1 = jnp.concatenate([s1_ref[...]] * 2, axis=1)
    s2 = jnp.concatenate([s2_ref[...]] * 2, axis=1)
    for n in range(2 * d // cw):
        acc = jnp.dot(xn, wqk_ref[:, n * cw:(n + 1) * cw], preferred_element_type=F32)
        r = acc * cos + pltpu.roll(acc, cw - 16, 1) * s1 + pltpu.roll(acc, 16, 1) * s2
        if n * cw < d:
            r = r * (0.5 * HEAD_W) ** -0.5
        qk_ref[:, n * cw:(n + 1) * cw] = r.astype(BF16)
    for n in range(d // cw):
        vt = lax.dot_general(wvt_ref[n * cw:(n + 1) * cw, :], xn, (((1,), (1,)), ((), ())),
                             preferred_element_type=F32)
        vt_ref[n * cw:(n + 1) * cw, :] = vt.astype(BF16)


def _qkv(h, g, mod, wqk, wvt, cos, s1, s2, *, n_lat, tt):
    bsz, t, d = h.shape
    kern = functools.partial(_qkv_kernel, n_lat=n_lat, ctx_row=bsz)
    return pl.pallas_call(
        kern,
        grid=(bsz, t // tt),
        in_specs=[pl.BlockSpec((None, tt, d), lambda b, i: (b, i, 0)),
                  pl.BlockSpec((1, d), lambda b, i: (0, 0)),
                  pl.BlockSpec((MOD_ROWS, d), lambda b, i: (0, 0)),
                  pl.BlockSpec((MOD_ROWS, d), lambda b, i: (0, 1)),
                  pl.BlockSpec((d, 2 * d), lambda b, i: (0, 0)),
                  pl.BlockSpec((d, d), lambda b, i: (0, 0)),
                  pl.BlockSpec((tt, HEAD_W), lambda b, i: (i, 0)),
                  pl.BlockSpec((tt, HEAD_W), lambda b, i: (i, 0)),
                  pl.BlockSpec((tt, HEAD_W), lambda b, i: (i, 0))],
        out_specs=[pl.BlockSpec((None, tt, 2 * d), lambda b, i: (b, i, 0)),
                   pl.BlockSpec((None, d, tt), lambda b, i: (b, 0, i))],
        out_shape=[jax.ShapeDtypeStruct((bsz, t, 2 * d), BF16),
                   jax.ShapeDtypeStruct((bsz, d, t), BF16)],
        compiler_params=_cparams(("arbitrary", "arbitrary")),
        name="qkv_rope",
    )(h, g, mod, mod, wqk, wvt, cos, s1, s2)


def _rope_tables(n_lat, n_ctx):
    freqs = HEAD_W // 8
    rows = n_lat // GRID_W
    row = jnp.repeat(jnp.arange(rows), GRID_W).astype(F32)
    col = jnp.tile(jnp.arange(GRID_W), rows).astype(F32)
    inv = ROPE_THETA ** (-(jnp.arange(freqs, dtype=F32) * 2.0) / (2 * freqs))
    lane = jnp.arange(HEAD_W)
    dd = lane % (HEAD_W // 2)
    axis = dd // (2 * freqs)
    half = (dd % (2 * freqs)) // freqs
    f = dd % freqs
    pos = jnp.where(axis[None, :] == 0, row[:, None], col[:, None])
    ang = pos * inv[f][None, :]
    cos = jnp.cos(ang)
    sin = jnp.sin(ang)
    s1 = jnp.where(half[None, :] == 0, -sin, 0.0)
    s2 = jnp.where(half[None, :] == 1, sin, 0.0)
    pad = lambda a, v: jnp.concatenate([a, jnp.full((n_ctx, HEAD_W), v, F32)], axis=0)
    return pad(cos, 1.0), pad(s1, 0.0), pad(s2, 0.0)


def _attn_kernel(lam_ref, sg_ref, q_ref, k_ref, vt_ref, o_ref, s_scr, *, kc, lam_init):
    tq = q_ref.shape[0]
    tk = k_ref.shape[0]
    q = q_ref[...].astype(F32)
    lane = lax.broadcasted_iota(I32, q.shape, 1)
    qbd = jnp.concatenate([jnp.where(lane < HEAD_W // 2, q, 0.0),
                           jnp.where(lane >= HEAD_W // 2, q, 0.0)], axis=0).astype(BF16)
    nchunks = tk // kc
    m8 = jnp.full((SUBLANES, 2 * tq), -jnp.inf, F32)
    for c in range(nchunks):
        s = lax.dot_general(k_ref[c * kc:(c + 1) * kc, :], qbd, (((1,), (1,)), ((), ())),
                            preferred_element_type=F32)
        s_scr[c * kc:(c + 1) * kc, :] = s
        m8 = jnp.maximum(m8, jnp.max(s.reshape(kc // SUBLANES, SUBLANES, 2 * tq), axis=0))
    m = jnp.max(m8, axis=0, keepdims=True)
    l8 = jnp.zeros((SUBLANES, 2 * tq), F32)
    acc = jnp.zeros((HEAD_W, 2 * tq), F32)
    for c in range(nchunks):
        e = jnp.exp(s_scr[c * kc:(c + 1) * kc, :] - m)
        l8 = l8 + jnp.sum(e.reshape(kc // SUBLANES, SUBLANES, 2 * tq), axis=0)
        acc = acc + jnp.dot(vt_ref[:, c * kc:(c + 1) * kc], e.astype(BF16),
                            preferred_element_type=F32)
    r = 1.0 / jnp.sum(l8, axis=0, keepdims=True)
    lv = lam_ref[...]
    lam = (jnp.exp(jnp.sum(lv[0:1, :] * lv[1:2, :], axis=1, keepdims=True))
           - jnp.exp(jnp.sum(lv[2:3, :] * lv[3:4, :], axis=1, keepdims=True)) + lam_init)
    ot = acc[:, :tq] * r[:, :tq] - lam * (acc[:, tq:] * r[:, tq:])
    ms = jnp.mean(ot * ot, axis=0, keepdims=True)
    ot = (ot * lax.rsqrt(ms + EPS)) * sg_ref[...] * (1.0 - lam_init)
    o_ref[...] = ot.T.astype(BF16)


def _attention(qk, vt, lamv, sg, o_prev, *, q_row0, n_q, k_row0, n_k, tq, kc, lam_init):
    bsz, t, d2 = qk.shape
    d = d2 // 2
    heads = d // HEAD_W
    qb0 = q_row0 // tq
    kb0 = k_row0 // n_k
    kern = functools.partial(_attn_kernel, kc=kc, lam_init=lam_init)
    in_specs = [pl.BlockSpec((4, HEAD_W // 2), lambda b, h, i: (0, 0)),
                pl.BlockSpec((HEAD_W, 1), lambda b, h, i: (0, 0)),
                pl.BlockSpec((None, tq, HEAD_W), lambda b, h, i: (b, qb0 + i, h)),
                pl.BlockSpec((None, n_k, HEAD_W), lambda b, h, i: (b, kb0, heads + h)),
                pl.BlockSpec((None, HEAD_W, n_k), lambda b, h, i: (b, h, kb0))]
    args = [lamv, sg, qk, qk, vt]
    aliases = {}
    if o_prev is not None:
        in_specs.append(pl.BlockSpec(memory_space=pl.ANY))
        args.append(o_prev)
        aliases = {5: 0}
        body = lambda a, s_, q_, k_, v_, prev_, o_, scr_: kern(a, s_, q_, k_, v_, o_, scr_)
    else:
        body = kern
    return pl.pallas_call(
        body,
        grid=(bsz, heads, n_q // tq),
        in_specs=in_specs,
        out_specs=pl.BlockSpec((None, tq, HEAD_W), lambda b, h, i: (b, qb0 + i, h)),
        out_shape=jax.ShapeDtypeStruct((bsz, t, d), BF16),
        scratch_shapes=[pltpu.VMEM((n_k, 2 * tq), F32)],
        input_output_aliases=aliases,
        compiler_params=_cparams(("arbitrary", "arbitrary", "arbitrary")),
        name="diff_attn_ctx" if o_prev is not None else "diff_attn_lat",
    )(*args)


def _proj_res_kernel(u_ref, w_ref, h_ref, gate_ref, o_ref, *, n_lat, ctx_row, nc):
    b = pl.program_id(0)
    i = pl.program_id(1)
    tt, d = h_ref.shape
    gate = _row_mod(gate_ref, b, ctx_row, i * tt, tt, n_lat)
    u = u_ref[...]
    for n in range(d // nc):
        y = jnp.dot(u, w_ref[:, n * nc:(n + 1) * nc], preferred_element_type=F32)
        o_ref[:, n * nc:(n + 1) * nc] = (h_ref[:, n * nc:(n + 1) * nc]
                                         + gate[:, n * nc:(n + 1) * nc] * y)


def _proj_res(u, w, h, mod, *, gate_chunk, n_rows, n_lat, tt):
    bsz, _, k = u.shape
    d = h.shape[-1]
    kern = functools.partial(_proj_res_kernel, n_lat=n_lat, ctx_row=bsz, nc=256)
    return pl.pallas_call(
        kern,
        grid=(bsz, n_rows // tt),
        in_specs=[pl.BlockSpec((None, tt, k), lambda b, i: (b, i, 0)),
                  pl.BlockSpec((k, d), lambda b, i: (0, 0)),
                  pl.BlockSpec((None, tt, d), lambda b, i: (b, i, 0)),
                  pl.BlockSpec((MOD_ROWS, d), lambda b, i: (0, gate_chunk))],
        out_specs=pl.BlockSpec((None, tt, d), lambda b, i: (b, i, 0)),
        out_shape=jax.ShapeDtypeStruct((bsz, n_rows, d), F32),
        compiler_params=_cparams(("arbitrary", "arbitrary")),
        name="proj_residual",
    )(u, w, h, mod)


def _lru_in_kernel(h_ref, g_ref, sh_ref, sc_ref, w_ref, bias_ref, gy_ref, xr_ref, *,
                   n_lat, ctx_row, nc):
    b = pl.program_id(0)
    i = pl.program_id(1)
    tt, d = h_ref.shape
    shift = _row_mod(sh_ref, b, ctx_row, i * tt, tt, n_lat)
    scale = _row_mod(sc_ref, b, ctx_row, i * tt, tt, n_lat)
    xn = _ln_mod(h_ref[...], g_ref[...], shift, scale).astype(BF16)
    for n in range(d // nc):
        y = jnp.dot(xn, w_ref[:, n * nc:(n + 1) * nc], preferred_element_type=F32)
        y = y + bias_ref[:, n * nc:(n + 1) * nc]
        gy_ref[:, n * nc:(n + 1) * nc] = _gelu_tanh(y).astype(BF16)
    for n in range(d // nc):
        x = jnp.dot(xn, w_ref[:, d + n * nc:d + (n + 1) * nc], preferred_element_type=F32)
        xr_ref[:, n * nc:(n + 1) * nc] = x + bias_ref[:, d + n * nc:d + (n + 1) * nc]


def _lru_in(h, g, mod, w_in, b_in, *, n_lat, tt):
    bsz, t, d = h.shape
    kern = functools.partial(_lru_in_kernel, n_lat=n_lat, ctx_row=bsz, nc=256)
    return pl.pallas_call(
        kern,
        grid=(bsz, t // tt),
        in_specs=[pl.BlockSpec((None, tt, d), lambda b, i: (b, i, 0)),
                  pl.BlockSpec((1, d), lambda b, i: (0, 0)),
                  pl.BlockSpec((MOD_ROWS, d), lambda b, i: (0, 0)),
                  pl.BlockSpec((MOD_ROWS, d), lambda b, i: (0, 1)),
                  pl.BlockSpec((d, 2 * d), lambda b, i: (0, 0)),
                  pl.BlockSpec((1, 2 * d), lambda b, i: (0, 0))],
        out_specs=[pl.BlockSpec((None, tt, d), lambda b, i: (b, i, 0)),
                   pl.BlockSpec((None, tt, d), lambda b, i: (b, i, 0))],
        out_shape=[jax.ShapeDtypeStruct((bsz, t, d), BF16),
                   jax.ShapeDtypeStruct((bsz, t, d), F32)],
        compiler_params=_cparams(("arbitrary", "arbitrary")),
        name="lru_in_proj",
    )(h, g, mod, mod, w_in, b_in)


def _group_scan(a, bt, reverse):
    rows = a.shape[0]
    sub = lax.broadcasted_iota(I32, a.shape, 0) % SUBLANES
    for s in (1, 2, 4):
        if reverse:
            a_sh = pltpu.roll(a, rows - s, 0)
            b_sh = pltpu.roll(bt, rows - s, 0)
            ok = sub < SUBLANES - s
        else:
            a_sh = pltpu.roll(a, s, 0)
            b_sh = pltpu.roll(bt, s, 0)
            ok = sub >= s
        bt = jnp.where(ok, a * b_sh + bt, bt)
        a = jnp.where(ok, a * a_sh, a)
    return a, bt


def _lru_kernel(xr_ref, gy_ref, cw_ref, cb_ref, wg_ref, bg_ref, lam_ref, u_ref,
                xp_scr, xc_scr, h_scr, *, n_lat, rc):
    t, cw = xr_ref.shape
    n_ctx = t - n_lat
    pad = SUBLANES
    zpad = jnp.zeros((pad, cw), F32)
    xp_scr[0:pad, :] = zpad
    xp_scr[pad:pad + n_lat, :] = xr_ref[0:n_lat, :]
    xp_scr[pad + n_lat:2 * pad + n_lat, :] = zpad
    xp_scr[2 * pad + n_lat:2 * pad + t, :] = xr_ref[n_lat:t, :]
    xp_scr[2 * pad + t:3 * pad + t, :] = zpad
    for (src0, dst0, n) in ((pad, 0, n_lat), (2 * pad + n_lat, n_lat, n_ctx)):
        acc = cb_ref[...] + cw_ref[0:1, :] * xp_scr[src0 - 2:src0 - 2 + n, :]
        for k in range(1, CONV_W):
            acc = acc + cw_ref[k:k + 1, :] * xp_scr[src0 - 2 + k:src0 - 2 + k + n, :]
        xc_scr[dst0:dst0 + n, :] = acc
    lam = lam_ref[...]
    sp = jnp.maximum(-lam, 0.0) + jnp.log(1.0 + jnp.exp(-jnp.abs(lam)))
    nchunks = t // rc
    ngroups = rc // SUBLANES

    def chunk(c, carry, d, reverse):
        base = pl.multiple_of(c * rc, rc)
        xc = xc_scr[pl.ds(base, rc), :]
        xb = xc.astype(BF16)
        a_parts, b_parts = [], []
        for kb in range(cw // LRU_BW):
            lo = kb * LRU_BW
            g = jnp.dot(xb[:, lo:lo + LRU_BW], wg_ref[d, kb], preferred_element_type=F32)
            g = g + bg_ref[d, kb]
            r = _sigmoid(g[:, :LRU_BW])
            ig = _sigmoid(g[:, LRU_BW:])
            a = jnp.exp((-LRU_C * sp[d:d + 1, lo:lo + LRU_BW]) * r)
            a_parts.append(a)
            b_parts.append(jnp.sqrt(1.0 - a * a) * (ig * xc[:, lo:lo + LRU_BW]))
        a = jnp.concatenate(a_parts, axis=1)
        bt = jnp.concatenate(b_parts, axis=1)
        a, bt = _group_scan(a, bt, reverse)
        a3 = a.reshape(ngroups, SUBLANES, cw)
        b3 = bt.reshape(ngroups, SUBLANES, cw)
        outs = [None] * ngroups
        order = range(ngroups - 1, -1, -1) if reverse else range(ngroups)
        for gidx in order:
            hg = b3[gidx] + a3[gidx] * carry
            outs[gidx] = hg
            carry = hg[0:1, :] if reverse else hg[SUBLANES - 1:SUBLANES, :]
        hs = jnp.concatenate(outs, axis=0)
        if reverse:
            h_scr[pl.ds(base, rc), :] = h_scr[pl.ds(base, rc), :] + hs
        else:
            h_scr[pl.ds(base, rc), :] = hs
        return carry

    n_lat_chunks = n_lat // rc
    carry = jnp.zeros((1, cw), F32)
    carry = lax.fori_loop(n_lat_chunks, nchunks, lambda c, cr: chunk(c, cr, 0, False), carry)
    carry = lax.fori_loop(0, n_lat_chunks, lambda c, cr: chunk(c, cr, 0, False), carry)
    carry = jnp.zeros((1, cw), F32)
    lax.fori_loop(0, nchunks, lambda j, cr: chunk(nchunks - 1 - j, cr, 1, True), carry)
    u_ref[...] = (gy_ref[...].astype(F32) * h_scr[0:n_lat, :]).astype(BF16)


def _lru_core(xr, gy, conv_w, conv_b, w_gates, b_gates, lam, *, n_lat, cw, rc):
    bsz, t, d = xr.shape
    nb = cw // LRU_BW
    kern = functools.partial(_lru_kernel, n_lat=n_lat, rc=rc)
    return pl.pallas_call(
        kern,
        grid=(bsz, d // cw),
        in_specs=[pl.BlockSpec((None, t, cw), lambda b, k: (b, 0, k)),
                  pl.BlockSpec((None, n_lat, cw), lambda b, k: (b, 0, k)),
                  pl.BlockSpec((CONV_W, cw), lambda b, k: (0, k)),
                  pl.BlockSpec((1, cw), lambda b, k: (0, k)),
                  pl.BlockSpec((2, nb, LRU_BW, 2 * LRU_BW), lambda b, k: (0, k, 0, 0)),
                  pl.BlockSpec((2, nb, 1, 2 * LRU_BW), lambda b, k: (0, k, 0, 0)),
                  pl.BlockSpec((2, cw), lambda b, k: (0, k))],
        out_specs=pl.BlockSpec((None, n_lat, cw), lambda b, k: (b, 0, k)),
        out_shape=jax.ShapeDtypeStruct((bsz, n_lat, d), BF16),
        scratch_shapes=[pltpu.VMEM((t + 3 * SUBLANES, cw), F32),
                        pltpu.VMEM((t, cw), F32),
                        pltpu.VMEM((t, cw), F32)],
        compiler_params=_cparams(("arbitrary", "arbitrary")),
        name="lru_core",
    )(xr, gy, conv_w, conv_b, w_gates, b_gates, lam)


def _norm_probs_kernel(h_ref, g_ref, sh_ref, sc_ref, wr_ref, m_ref, p_ref, *,
                       n_lat, ctx_row, n_exp):
    b = pl.program_id(0)
    i = pl.program_id(1)
    tt, d = h_ref.shape
    shift = _row_mod(sh_ref, b, ctx_row, i * tt, tt, n_lat)
    scale = _row_mod(sc_ref, b, ctx_row, i * tt, tt, n_lat)
    m = _ln_mod(h_ref[...], g_ref[...], shift, scale)
    m_ref[...] = m.astype(BF16)
    logits = jnp.dot(m, wr_ref[...], preferred_element_type=F32, precision=lax.Precision.HIGHEST)
    lane = lax.broadcasted_iota(I32, logits.shape, 1)
    lg = jnp.where(lane < n_exp, logits, -jnp.inf)
    ex = jnp.exp(lg - jnp.max(lg, axis=1, keepdims=True))
    p_ref[...] = ex / jnp.sum(ex, axis=1, keepdims=True)


def _norm_probs(h, g, mod, wr, *, n_lat, n_exp, tt):
    bsz, t, d = h.shape
    kern = functools.partial(_norm_probs_kernel, n_lat=n_lat, ctx_row=bsz, n_exp=n_exp)
    return pl.pallas_call(
        kern,
        grid=(bsz, t // tt),
        in_specs=[pl.BlockSpec((None, tt, d), lambda b, i: (b, i, 0)),
                  pl.BlockSpec((1, d), lambda b, i: (0, 0)),
                  pl.BlockSpec((MOD_ROWS, d), lambda b, i: (0, 3)),
                  pl.BlockSpec((MOD_ROWS, d), lambda b, i: (0, 4)),
                  pl.BlockSpec((d, LANES), lambda b, i: (0, 0))],
        out_specs=[pl.BlockSpec((None, tt, d), lambda b, i: (b, i, 0)),
                   pl.BlockSpec((None, tt, LANES), lambda b, i: (b, i, 0))],
        out_shape=[jax.ShapeDtypeStruct((bsz, t, d), BF16),
                   jax.ShapeDtypeStruct((bsz, t, LANES), F32)],
        compiler_params=_cparams(("arbitrary", "arbitrary")),
        name="moe_norm_probs",
    )(h, g, mod, mod, wr)


def _select(p, ltri, cap):
    n_tok = p.shape[0]

    def count(mask):
        c = jnp.where(mask, 1.0, 0.0).reshape(n_tok // SUBLANES, SUBLANES, LANES)
        return jnp.sum(jnp.sum(c, axis=0), axis=0, keepdims=True)

    def body(i, thr):
        cand = thr | jnp.left_shift(jnp.int32(1), 29 - i)
        return jnp.where(count(p >= pltpu.bitcast(cand, F32)) >= cap, cand, thr)

    thr = lax.fori_loop(0, 30, body, jnp.zeros((1, LANES), I32))
    gt = p >= pltpu.bitcast(thr + 1, F32)
    eq = (p >= pltpu.bitcast(thr, F32)) & jnp.logical_not(gt)
    need = cap - count(gt)
    pre_eq = jnp.dot(ltri, jnp.where(eq, 1.0, 0.0).astype(BF16), preferred_element_type=F32)
    sel = gt | (eq & (pre_eq < need))
    slot = jnp.dot(ltri, jnp.where(sel, 1.0, 0.0).astype(BF16), preferred_element_type=F32)
    return jnp.where(sel, slot, -1.0), jnp.where(sel, p, 0.0)


def _select_kernel(p_ref, slot_ref, gate_ref, slot_t_ref, ltri_scr, *, n_lat, n_exp):
    b = pl.program_id(0)
    t = p_ref.shape[0]
    n_ctx = t - n_lat

    @pl.when(b == 0)
    def _():
        r = lax.broadcasted_iota(I32, (n_lat, n_lat), 0)
        c = lax.broadcasted_iota(I32, (n_lat, n_lat), 1)
        ltri_scr[...] = jnp.where(c < r, 1.0, 0.0).astype(BF16)

    slot, gate = _select(p_ref[0:n_lat, :], ltri_scr[...], EC_CAPACITY * n_lat // n_exp)
    slot_ref[0:n_lat, :] = slot
    gate_ref[0:n_lat, :] = gate
    slot_t_ref[:, 0:n_lat] = slot.T
    if n_ctx:
        slot, gate = _select(p_ref[n_lat:t, :], ltri_scr[0:n_ctx, 0:n_ctx],
                             EC_CAPACITY * n_ctx // n_exp)
        slot_ref[n_lat:t, :] = slot
        gate_ref[n_lat:t, :] = gate
        slot_t_ref[:, n_lat:t] = slot.T


def _select_tokens(p, *, n_lat, n_exp):
    bsz, t, _ = p.shape
    kern = functools.partial(_select_kernel, n_lat=n_lat, n_exp=n_exp)
    return pl.pallas_call(
        kern,
        grid=(bsz,),
        in_specs=[pl.BlockSpec((None, t, LANES), lambda b: (b, 0, 0))],
        out_specs=[pl.BlockSpec((None, t, LANES), lambda b: (b, 0, 0)),
                   pl.BlockSpec((None, t, LANES), lambda b: (b, 0, 0)),
                   pl.BlockSpec((None, LANES, t), lambda b: (b, 0, 0))],
        out_shape=[jax.ShapeDtypeStruct((bsz, t, LANES), F32),
                   jax.ShapeDtypeStruct((bsz, t, LANES), F32),
                   jax.ShapeDtypeStruct((bsz, LANES, t), F32)],
        scratch_shapes=[pltpu.VMEM((n_lat, n_lat), BF16)],
        compiler_params=_cparams(("arbitrary",)),
        name="moe_select",
    )(p)


def _gather_kernel(m_ref, slot_t_ref, xg_ref, *, n_lat, cap_lat, cap_ctx):
    j = pl.program_id(1)
    eg = xg_ref.shape[0]
    t = m_ref.shape[0]
    for ee in range(eg):
        row = slot_t_ref[pl.ds(j * eg + ee, 1), :]
        sl = lax.broadcasted_iota(I32, (cap_lat, 1), 0).astype(F32)
        p = jnp.where(row[:, 0:n_lat] == sl, 1.0, 0.0).astype(BF16)
        xg_ref[ee, 0:cap_lat, :] = jnp.dot(p, m_ref[0:n_lat, :],
                                           preferred_element_type=F32).astype(BF16)
        if cap_ctx:
            sc = lax.broadcasted_iota(I32, (cap_ctx, 1), 0).astype(F32)
            pc = jnp.where(row[:, n_lat:t] == sc, 1.0, 0.0).astype(BF16)
            xg_ref[ee, cap_lat:cap_lat + cap_ctx, :] = jnp.dot(
                pc, m_ref[n_lat:t, :], preferred_element_type=F32).astype(BF16)


def _gather(m, slot_t, *, n_lat, n_exp, eg):
    bsz, t, d = m.shape
    cap_lat = EC_CAPACITY * n_lat // n_exp
    cap_ctx = EC_CAPACITY * (t - n_lat) // n_exp
    r = cap_lat + cap_ctx
    kern = functools.partial(_gather_kernel, n_lat=n_lat, cap_lat=cap_lat, cap_ctx=cap_ctx)
    return pl.pallas_call(
        kern,
        grid=(bsz, n_exp // eg),
        in_specs=[pl.BlockSpec((None, t, d), lambda b, j: (b, 0, 0)),
                  pl.BlockSpec((None, LANES, t), lambda b, j: (b, 0, 0))],
        out_specs=pl.BlockSpec((eg, None, r, d), lambda b, j: (j, b, 0, 0)),
        out_shape=jax.ShapeDtypeStruct((n_exp, bsz, r, d), BF16),
        compiler_params=_cparams(("arbitrary", "arbitrary")),
        name="moe_gather",
    )(m, slot_t)


def _ffn_kernel(x_ref, wg_ref, wu_ref, wd_ref, y_ref, acc_scr, *, rc):
    j = pl.program_id(1)
    nj = pl.num_programs(1)
    rows = x_ref.shape[0]

    @pl.when(j == 0)
    def _():
        acc_scr[...] = jnp.zeros_like(acc_scr)

    wg = wg_ref[...].astype(BF16)
    wu = wu_ref[...].astype(BF16)
    wd = wd_ref[...].astype(BF16)

    def body(c, carry):
        base = pl.multiple_of(c * rc, rc)
        x = x_ref[pl.ds(base, rc), :]
        g = jnp.dot(x, wg, preferred_element_type=F32)
        u = jnp.dot(x, wu, preferred_element_type=F32)
        hid = ((g * _sigmoid(g)) * u).astype(BF16)
        acc_scr[pl.ds(base, rc), :] += jnp.dot(hid, wd, preferred_element_type=F32)
        return carry

    lax.fori_loop(0, rows // rc, body, 0)

    @pl.when(j == nj - 1)
    def _():
        y_ref[...] = acc_scr[...].astype(BF16)


def _ffn(xg, w_gate_up, w_down, *, fc, rc):
    n_exp, rows, d = xg.shape
    f = w_down.shape[1]
    nj = f // fc
    kern = functools.partial(_ffn_kernel, rc=rc)
    return pl.pallas_call(
        kern,
        grid=(n_exp, nj),
        in_specs=[pl.BlockSpec((None, rows, d), lambda e, j: (e, 0, 0)),
                  pl.BlockSpec((None, d, fc), lambda e, j: (e, 0, j)),
                  pl.BlockSpec((None, d, fc), lambda e, j: (e, 0, nj + j)),
                  pl.BlockSpec((None, fc, d), lambda e, j: (e, j, 0))],
        out_specs=pl.BlockSpec((None, rows, d), lambda e, j: (e, 0, 0)),
        out_shape=jax.ShapeDtypeStruct((n_exp, rows, d), BF16),
        scratch_shapes=[pltpu.VMEM((rows, d), F32)],
        compiler_params=_cparams(("arbitrary", "arbitrary")),
        name="moe_ffn",
    )(xg, w_gate_up, w_gate_up, w_down)


def _combine_kernel(y_ref, slot_ref, gate_ref, h_ref, g2_ref, fg_ref, o_ref, *,
                    n_lat, ctx_row, cap_lat, cap_ctx, final_norm):
    b = pl.program_id(0)
    i = pl.program_id(1)
    tt, d = h_ref.shape
    n_exp = y_ref.shape[0]
    n_lat_tiles = n_lat // tt

    def finish(acc):
        gate2 = _row_mod(g2_ref, b, ctx_row, i * tt, tt, n_lat)
        out = h_ref[...] + gate2 * acc
        if final_norm:
            ms = jnp.mean(out * out, axis=-1, keepdims=True)
            out = (out * lax.rsqrt(ms + EPS)) * fg_ref[...]
        o_ref[...] = out

    @pl.when(i < n_lat_tiles)
    def _():
        slot = slot_ref[...]
        gate = gate_ref[...]
        iota_c = lax.broadcasted_iota(I32, (1, cap_lat), 1).astype(F32)
        acc = jnp.zeros((tt, d), F32)
        for e in range(n_exp):
            pt = jnp.where(slot[:, e:e + 1] == iota_c, gate[:, e:e + 1], 0.0).astype(BF16)
            acc = acc + jnp.dot(pt, y_ref[e, 0:cap_lat, :], preferred_element_type=F32)
        finish(acc)

    if cap_ctx:
        @pl.when(i >= n_lat_tiles)
        def _():
            slot = slot_ref[...]
            gate = gate_ref[...]
            iota_c = lax.broadcasted_iota(I32, (1, n_exp * cap_ctx), 1).astype(F32)
            pt = jnp.zeros((tt, n_exp * cap_ctx), F32)
            for e in range(n_exp):
                pt = pt + jnp.where(slot[:, e:e + 1] + float(e * cap_ctx) == iota_c,
                                    gate[:, e:e + 1], 0.0)
            yc = y_ref[:, cap_lat:cap_lat + cap_ctx, :].reshape(n_exp * cap_ctx, d)
            finish(jnp.dot(pt.astype(BF16), yc, preferred_element_type=F32))


def _combine(y, slot, gate, h, mod, final_g, *, n_lat, n_out, tt, final_norm):
    n_exp, bsz, r, d = y.shape
    t = slot.shape[1]
    cap_lat = EC_CAPACITY * n_lat // n_exp
    cap_ctx = EC_CAPACITY * (t - n_lat) // n_exp
    kern = functools.partial(_combine_kernel, n_lat=n_lat, ctx_row=bsz, cap_lat=cap_lat,
                             cap_ctx=cap_ctx, final_norm=final_norm)
    return pl.pallas_call(
        kern,
        grid=(bsz, n_out // tt),
        in_specs=[pl.BlockSpec((n_exp, None, r, d), lambda b, i: (0, b, 0, 0)),
                  pl.BlockSpec((None, tt, LANES), lambda b, i: (b, i, 0)),
                  pl.BlockSpec((None, tt, LANES), lambda b, i: (b, i, 0)),
                  pl.BlockSpec((None, tt, d), lambda b, i: (b, i, 0)),
                  pl.BlockSpec((MOD_ROWS, d), lambda b, i: (0, 5)),
                  pl.BlockSpec((1, d), lambda b, i: (0, 0))],
        out_specs=pl.BlockSpec((None, tt, d), lambda b, i: (b, i, 0)),
        out_shape=jax.ShapeDtypeStruct((bsz, n_out, d), F32),
        compiler_params=_cparams(("arbitrary", "arbitrary")),
        name="moe_combine",
    )(y, slot, gate, h, mod, final_g)


def _moe(h, g2, mod, w_router, w_gate_up, w_down, final_g, *, n_lat, final_norm):
    bsz, t, d = h.shape
    n_exp = w_router.shape[-1]
    wr = jnp.pad(w_router, ((0, 0), (0, LANES - n_exp)))
    m, p = _norm_probs(h, g2, mod, wr, n_lat=n_lat, n_exp=n_exp, tt=_pick(t, (768, 512, 256)))
    slot, gate, slot_t = _select_tokens(p, n_lat=n_lat, n_exp=n_exp)
    xg = _gather(m, slot_t, n_lat=n_lat, n_exp=n_exp, eg=4)
    r = xg.shape[2]
    y = _ffn(xg.reshape(n_exp, bsz * r, d), w_gate_up, w_down, fc=256,
             rc=_pick(bsz * r, (256, 128, 64, 32, 16)))
    return _combine(y.reshape(n_exp, bsz, r, d), slot, gate, h, mod, final_g,
                    n_lat=n_lat, n_out=t, tt=256, final_norm=final_norm)


def kernel(x, c, ctx, c_ctx, ada_w, ada_b, norm1_g, norm2_g, final_g, attn_w_qkv, attn_lq1, attn_lk1, attn_lq2, attn_lk2, attn_subln_g, attn_w_o, lru_w_in, lru_b_in, lru_conv_w, lru_conv_b, lru_w_gates, lru_b_gates, lru_lambda, lru_w_out, moe_w_router, moe_w_gate_up, moe_w_down):
    bsz, n_lat, d = x.shape
    n_ctx = ctx.shape[1]
    depth = ada_w.shape[0]
    assert bsz < MOD_ROWS and d % (2 * HEAD_W) == 0 and n_lat % GRID_W == 0

    cc = jnp.concatenate([c, c_ctx[None, :], jnp.zeros((MOD_ROWS - bsz - 1, d), F32)], axis=0)
    mods = _mod_tables(cc, ada_w, ada_b)
    h = jnp.concatenate([x, ctx], axis=1)
    row = lambda v: v.reshape(1, -1)

    for i in range(depth):
        last = i == depth - 1
        mod = mods[i]
        j = i // N_MIXERS
        if i % N_MIXERS == 0:
            lam_init = 0.8 - 0.6 * math.exp(-0.3 * i)
            w = attn_w_qkv[j]
            wqk = w[:, :2 * d].astype(BF16)
            wvt = w[:, 2 * d:].T.astype(BF16)
            cos, s1, s2 = _rope_tables(n_lat, n_ctx)
            t_all = n_lat + n_ctx
            qk, vt = _qkv(h, row(norm1_g[i]), mod, wqk, wvt, cos, s1, s2, n_lat=n_lat,
                          tt=_pick(t_all, (768, 256)))
            lamv = jnp.stack([attn_lq1[j], attn_lk1[j], attn_lq2[j], attn_lk2[j]], axis=0)
            sg = attn_subln_g[j].reshape(HEAD_W, 1)
            o = _attention(qk, vt, lamv, sg, None, q_row0=0, n_q=n_lat, k_row0=0,
                           n_k=n_lat + n_ctx, tq=_pick(n_lat, (512, 256)), kc=256,
                           lam_init=lam_init)
            n_rows = n_lat
            if not last:
                o = _attention(qk, vt, lamv, sg, o, q_row0=n_lat, n_q=n_ctx, k_row0=n_lat,
                               n_k=n_ctx, tq=n_ctx, kc=n_ctx, lam_init=lam_init)
                n_rows = n_lat + n_ctx
            h = _proj_res(o, attn_w_o[j].astype(BF16), h, mod, gate_chunk=2, n_rows=n_rows,
                          n_lat=n_lat, tt=_pick(n_rows, (768, 512, 256)))
        else:
            gy, xr = _lru_in(h, row(norm1_g[i]), mod, lru_w_in[j].astype(BF16),
                             row(lru_b_in[j]), n_lat=n_lat, tt=_pick(n_lat + n_ctx, (768, 256)))
            nblk = d // LRU_BW
            u = _lru_core(xr, gy, lru_conv_w[j], row(lru_conv_b[j]),
                          lru_w_gates[j].astype(BF16),
                          lru_b_gates[j].reshape(2, nblk, 1, 2 * LRU_BW), lru_lambda[j],
                          n_lat=n_lat, cw=256, rc=256)
            n_rows = n_lat if last else n_lat + n_ctx
            assert last, "context output of the recurrent mixer is only needed in non-final layers"
            h = _proj_res(u, lru_w_out[j].astype(BF16), h, mod, gate_chunk=2, n_rows=n_rows,
                          n_lat=n_lat, tt=_pick(n_rows, (1024, 512, 256)))
        h = _moe(h, row(norm2_g[i]), mod, moe_w_router[i], moe_w_gate_up[i], moe_w_down[i],
                 row(final_g), n_lat=n_lat, final_norm=last)
    return h
```

```python
import functools
import math

import jax
import jax.numpy as jnp
from jax import lax
from jax.experimental import pallas as pl
from jax.experimental.pallas import tpu as pltpu

F32 = jnp.float32
BF16 = jnp.bfloat16
I32 = jnp.int32

EPS = 1e-6
GRID_W = 64
ROPE_THETA = 10000.0
LRU_C = 8.0
EC_CAPACITY = 2
N_MIXERS = 2
HEAD_W = 128
LRU_BW = 128
CONV_W = 4
MOD_ROWS = 16
LANES = 128
SUBLANES = 8
VMEM_LIMIT = 56 * 2**20


def _pick(n, cands):
    return next(c for c in cands if n % c == 0)


def _cparams(sem, flags=None):
    return pltpu.CompilerParams(dimension_semantics=sem, vmem_limit_bytes=VMEM_LIMIT, flags=flags)


def _sigmoid(x):
    return 1.0 / (1.0 + jnp.exp(-x))


def _gelu_tanh(x):
    return 0.5 * x * (1.0 + jnp.tanh(math.sqrt(2.0 / math.pi) * (x + 0.044715 * (x * x * x))))


def _row_mod(mod_ref, b, ctx_row, row0, nrows, n_lat):
    mb = mod_ref[pl.ds(b, 1), :]
    mc = mod_ref[ctx_row:ctx_row + 1, :]
    rows = row0 + lax.broadcasted_iota(I32, (nrows, 1), 0)
    return jnp.where(rows < n_lat, mb, mc)


def _ln_mod(x, g, shift, scale):
    ms = jnp.mean(x * x, axis=-1, keepdims=True)
    y = (x * lax.rsqrt(ms + EPS)) * g
    return y * (1.0 + scale) + shift


def _mod_kernel(c_ref, w_ref, b_ref, o_ref):
    c = c_ref[...]
    s = c * _sigmoid(c)
    o_ref[...] = jnp.dot(s, w_ref[...], preferred_element_type=F32,
                         precision=lax.Precision.HIGHEST) + b_ref[...]


def _mod_tables(cc, ada_w, ada_b):
    depth, d, n6 = ada_w.shape
    tn = _pick(n6, (1536, 768, 384))
    return pl.pallas_call(
        _mod_kernel,
        grid=(depth, n6 // tn),
        in_specs=[pl.BlockSpec((MOD_ROWS, d), lambda i, j: (0, 0)),
                  pl.BlockSpec((None, d, tn), lambda i, j: (i, 0, j)),
                  pl.BlockSpec((None, 1, tn), lambda i, j: (i, 0, j))],
        out_specs=pl.BlockSpec((None, MOD_ROWS, tn), lambda i, j: (i, 0, j)),
        out_shape=jax.ShapeDtypeStruct((depth, MOD_ROWS, n6), F32),
        compiler_params=_cparams(("arbitrary", "arbitrary")),
        name="mod_tables",
    )(cc, ada_w, ada_b.reshape(depth, 1, n6))


def _qkv_kernel(h_ref, g_ref, sh_ref, sc_ref, wqk_ref, wvt_ref, cos_ref, s1_ref, s2_ref,
                qk_ref, vt_ref, *, n_lat, ctx_row):
    b = pl.program_id(0)
    i = pl.program_id(1)
    tt, d = h_ref.shape
    shift = _row_mod(sh_ref, b, ctx_row, i * tt, tt, n_lat)
    scale = _row_mod(sc_ref, b, ctx_row, i * tt, tt, n_lat)
    xn = _ln_mod(h_ref[...], g_ref[...], shift, scale).astype(BF16)
    cw = 2 * HEAD_W
    cos = jnp.concatenate([cos_ref[...]] * 2, axis=1)
    s1 = jnp.concatenate([s1_ref[...]] * 2, axis=1)
    s2 = jnp.concatenate([s2_ref[...]] * 2, axis=1)
    for n in range(2 * d // cw):
        acc = jnp.dot(xn, wqk_ref[:, n * cw:(n + 1) * cw], preferred_element_type=F32)
        r = acc * cos + pltpu.roll(acc, cw - 16, 1) * s1 + pltpu.roll(acc, 16, 1) * s2
        if n * cw < d:
            r = r * ((0.5 * HEAD_W) ** -0.5 * math.log2(math.e))
        qk_ref[:, n * cw:(n + 1) * cw] = r.astype(BF16)
    for n in range(d // cw):
        vt = lax.dot_general(wvt_ref[n * cw:(n + 1) * cw, :], xn, (((1,), (1,)), ((), ())),
                             preferred_element_type=F32)
        vt_ref[n * cw:(n + 1) * cw, :] = vt.astype(BF16)


def _qkv(h, g, mod, wqk, wvt, cos, s1, s2, *, n_lat, tt):
    bsz, t, d = h.shape
    kern = functools.partial(_qkv_kernel, n_lat=n_lat, ctx_row=bsz)
    return pl.pallas_call(
        kern,
        grid=(bsz, t // tt),
        in_specs=[pl.BlockSpec((None, tt, d), lambda b, i: (b, i, 0)),
                  pl.BlockSpec((1, d), lambda b, i: (0, 0)),
                  pl.BlockSpec((MOD_ROWS, d), lambda b, i: (0, 0)),
                  pl.BlockSpec((MOD_ROWS, d), lambda b, i: (0, 1)),
                  pl.BlockSpec((d, 2 * d), lambda b, i: (0, 0)),
                  pl.BlockSpec((d, d), lambda b, i: (0, 0)),
                  pl.BlockSpec((tt, HEAD_W), lambda b, i: (i, 0)),
                  pl.BlockSpec((tt, HEAD_W), lambda b, i: (i, 0)),
                  pl.BlockSpec((tt, HEAD_W), lambda b, i: (i, 0))],
        out_specs=[pl.BlockSpec((None, tt, 2 * d), lambda b, i: (b, i, 0)),
                   pl.BlockSpec((None, d, tt), lambda b, i: (b, 0, i))],
        out_shape=[jax.ShapeDtypeStruct((bsz, t, 2 * d), BF16),
                   jax.ShapeDtypeStruct((bsz, d, t), BF16)],
        compiler_params=_cparams(("arbitrary", "arbitrary")),
        name="qkv_rope",
    )(h, g, mod, mod, wqk, wvt, cos, s1, s2)


def _rope_tables(n_lat, n_ctx):
    freqs = HEAD_W // 8
    rows = n_lat // GRID_W
    row = jnp.repeat(jnp.arange(rows), GRID_W).astype(F32)
    col = jnp.tile(jnp.arange(GRID_W), rows).astype(F32)
    inv = ROPE_THETA ** (-(jnp.arange(freqs, dtype=F32) * 2.0) / (2 * freqs))
    lane = jnp.arange(HEAD_W)
    dd = lane % (HEAD_W // 2)
    axis = dd // (2 * freqs)
    half = (dd % (2 * freqs)) // freqs
    f = dd % freqs
    pos = jnp.where(axis[None, :] == 0, row[:, None], col[:, None])
    ang = pos * inv[f][None, :]
    cos = jnp.cos(ang)
    sin = jnp.sin(ang)
    s1 = jnp.where(half[None, :] == 0, -sin, 0.0)
    s2 = jnp.where(half[None, :] == 1, sin, 0.0)
    pad = lambda a, v: jnp.concatenate([a, jnp.full((n_ctx, HEAD_W), v, F32)], axis=0)
    return pad(cos, 1.0), pad(s1, 0.0), pad(s2, 0.0)


def _attn_kernel(lam_ref, sg_ref, q_ref, k_ref, vt_ref, o_ref, s_scr, *, kc, lam_init):
    tq = q_ref.shape[0]
    tk = k_ref.shape[0]
    nh = q_ref.shape[1] // HEAD_W
    nchunks = tk // kc
    lv = lam_ref[...]
    lam = (jnp.exp(jnp.sum(lv[0:1, :] * lv[1:2, :], axis=1, keepdims=True))
           - jnp.exp(jnp.sum(lv[2:3, :] * lv[3:4, :], axis=1, keepdims=True)) + lam_init)
    lane = lax.broadcasted_iota(I32, (tq, HEAD_W), 1)
    ms = []
    for hh in range(nh):
        q = q_ref[:, hh * HEAD_W:(hh + 1) * HEAD_W].astype(F32)
        qbd = jnp.concatenate([jnp.where(lane < HEAD_W // 2, q, 0.0),
                               jnp.where(lane >= HEAD_W // 2, q, 0.0)], axis=0).astype(BF16)
        m8 = jnp.full((SUBLANES, 2 * tq), -jnp.inf, F32)
        for c in range(nchunks):
            s = lax.dot_general(k_ref[c * kc:(c + 1) * kc, hh * HEAD_W:(hh + 1) * HEAD_W], qbd,
                                (((1,), (1,)), ((), ())), preferred_element_type=F32)
            s_scr[hh, c * kc:(c + 1) * kc, :] = s
            m8 = jnp.maximum(m8, jnp.max(s.reshape(kc // SUBLANES, SUBLANES, 2 * tq), axis=0))
        ms.append(jnp.max(m8, axis=0, keepdims=True))
    for hh in range(nh):
        l8 = jnp.zeros((SUBLANES, 2 * tq), F32)
        acc = jnp.zeros((HEAD_W, 2 * tq), F32)
        for c in range(nchunks):
            e = jnp.exp2(s_scr[hh, c * kc:(c + 1) * kc, :] - ms[hh])
            l8 = l8 + jnp.sum(e.reshape(kc // SUBLANES, SUBLANES, 2 * tq), axis=0)
            acc = acc + jnp.dot(vt_ref[hh * HEAD_W:(hh + 1) * HEAD_W, c * kc:(c + 1) * kc],
                                e.astype(BF16), preferred_element_type=F32)
        r = 1.0 / jnp.sum(l8, axis=0, keepdims=True)
        ot = acc[:, :tq] * r[:, :tq] - lam * (acc[:, tq:] * r[:, tq:])
        msq = jnp.mean(ot * ot, axis=0, keepdims=True)
        ot = (ot * lax.rsqrt(msq + EPS)) * sg_ref[...] * (1.0 - lam_init)
        o_ref[:, hh * HEAD_W:(hh + 1) * HEAD_W] = ot.T.astype(BF16)


def _attention(qk, vt, lamv, sg, o_prev, *, q_row0, n_q, k_row0, n_k, tq, kc, nh, lam_init):
    bsz, t, d2 = qk.shape
    d = d2 // 2
    hw = nh * HEAD_W
    heads = d // hw
    qb0 = q_row0 // tq
    kb0 = k_row0 // n_k
    kern = functools.partial(_attn_kernel, kc=kc, lam_init=lam_init)
    in_specs = [pl.BlockSpec((4, HEAD_W // 2), lambda b, h, i: (0, 0)),
                pl.BlockSpec((HEAD_W, 1), lambda b, h, i: (0, 0)),
                pl.BlockSpec((None, tq, hw), lambda b, h, i: (b, qb0 + i, h)),
                pl.BlockSpec((None, n_k, hw), lambda b, h, i: (b, kb0, heads + h)),
                pl.BlockSpec((None, hw, n_k), lambda b, h, i: (b, h, kb0))]
    args = [lamv, sg, qk, qk, vt]
    aliases = {}
    if o_prev is not None:
        in_specs.append(pl.BlockSpec(memory_space=pl.ANY))
        args.append(o_prev)
        aliases = {5: 0}
        body = lambda a, s_, q_, k_, v_, prev_, o_, scr_: kern(a, s_, q_, k_, v_, o_, scr_)
    else:
        body = kern
    return pl.pallas_call(
        body,
        grid=(bsz, heads, n_q // tq),
        in_specs=in_specs,
        out_specs=pl.BlockSpec((None, tq, hw), lambda b, h, i: (b, qb0 + i, h)),
        out_shape=jax.ShapeDtypeStruct((bsz, t, d), BF16),
        scratch_shapes=[pltpu.VMEM((nh, n_k, 2 * tq), F32)],
        input_output_aliases=aliases,
        compiler_params=_cparams(("arbitrary", "arbitrary", "arbitrary")),
        name="diff_attn_ctx" if o_prev is not None else "diff_attn_lat",
    )(*args)


def _proj_res_kernel(u_ref, w_ref, h_ref, gate_ref, o_ref, *, n_lat, ctx_row, nc):
    b = pl.program_id(0)
    i = pl.program_id(1)
    tt, d = h_ref.shape
    gate = _row_mod(gate_ref, b, ctx_row, i * tt, tt, n_lat)
    u = u_ref[...]
    for n in range(d // nc):
        y = jnp.dot(u, w_ref[:, n * nc:(n + 1) * nc], preferred_element_type=F32)
        o_ref[:, n * nc:(n + 1) * nc] = (h_ref[:, n * nc:(n + 1) * nc]
                                         + gate[:, n * nc:(n + 1) * nc] * y)


def _proj_res(u, w, h, mod, *, gate_chunk, n_rows, n_lat, tt):
    bsz, _, k = u.shape
    d = h.shape[-1]
    kern = functools.partial(_proj_res_kernel, n_lat=n_lat, ctx_row=bsz, nc=256)
    return pl.pallas_call(
        kern,
        grid=(bsz, n_rows // tt),
        in_specs=[pl.BlockSpec((None, tt, k), lambda b, i: (b, i, 0)),
                  pl.BlockSpec((k, d), lambda b, i: (0, 0)),
                  pl.BlockSpec((None, tt, d), lambda b, i: (b, i, 0)),
                  pl.BlockSpec((MOD_ROWS, d), lambda b, i: (0, gate_chunk))],
        out_specs=pl.BlockSpec((None, tt, d), lambda b, i: (b, i, 0)),
        out_shape=jax.ShapeDtypeStruct((bsz, n_rows, d), F32),
        compiler_params=_cparams(("arbitrary", "arbitrary")),
        name="proj_residual",
    )(u, w, h, mod)


def _lru_in_kernel(h_ref, g_ref, sh_ref, sc_ref, w_ref, bias_ref, gy_ref, xr_ref, *,
                   n_lat, ctx_row, nc):
    b = pl.program_id(0)
    i = pl.program_id(1)
    tt, d = h_ref.shape
    shift = _row_mod(sh_ref, b, ctx_row, i * tt, tt, n_lat)
    scale = _row_mod(sc_ref, b, ctx_row, i * tt, tt, n_lat)
    xn = _ln_mod(h_ref[...], g_ref[...], shift, scale).astype(BF16)
    for n in range(d // nc):
        y = jnp.dot(xn, w_ref[:, n * nc:(n + 1) * nc], preferred_element_type=F32)
        y = y + bias_ref[:, n * nc:(n + 1) * nc]
        gy_ref[:, n * nc:(n + 1) * nc] = _gelu_tanh(y).astype(BF16)
    for n in range(d // nc):
        x = jnp.dot(xn, w_ref[:, d + n * nc:d + (n + 1) * nc], preferred_element_type=F32)
        xr_ref[:, n * nc:(n + 1) * nc] = x + bias_ref[:, d + n * nc:d + (n + 1) * nc]


def _lru_in(h, g, mod, w_in, b_in, *, n_lat, tt):
    bsz, t, d = h.shape
    kern = functools.partial(_lru_in_kernel, n_lat=n_lat, ctx_row=bsz, nc=256)
    return pl.pallas_call(
        kern,
        grid=(bsz, t // tt),
        in_specs=[pl.BlockSpec((None, tt, d), lambda b, i: (b, i, 0)),
                  pl.BlockSpec((1, d), lambda b, i: (0, 0)),
                  pl.BlockSpec((MOD_ROWS, d), lambda b, i: (0, 0)),
                  pl.BlockSpec((MOD_ROWS, d), lambda b, i: (0, 1)),
                  pl.BlockSpec((d, 2 * d), lambda b, i: (0, 0)),
                  pl.BlockSpec((1, 2 * d), lambda b, i: (0, 0))],
        out_specs=[pl.BlockSpec((None, tt, d), lambda b, i: (b, i, 0)),
                   pl.BlockSpec((None, tt, d), lambda b, i: (b, i, 0))],
        out_shape=[jax.ShapeDtypeStruct((bsz, t, d), BF16),
                   jax.ShapeDtypeStruct((bsz, t, d), F32)],
        compiler_params=_cparams(("arbitrary", "arbitrary")),
        name="lru_in_proj",
    )(h, g, mod, mod, w_in, b_in)


def _group_scan(a, bt, reverse):
    rows = a.shape[0]
    sub = lax.broadcasted_iota(I32, a.shape, 0) % SUBLANES
    for s in (1, 2, 4):
        if reverse:
            a_sh = pltpu.roll(a, rows - s, 0)
            b_sh = pltpu.roll(bt, rows - s, 0)
            ok = sub < SUBLANES - s
        else:
            a_sh = pltpu.roll(a, s, 0)
            b_sh = pltpu.roll(bt, s, 0)
            ok = sub >= s
        bt = jnp.where(ok, a * b_sh + bt, bt)
        a = jnp.where(ok, a * a_sh, a)
    return a, bt


def _lru_kernel(xr_ref, gy_ref, cw_ref, cb_ref, wg_ref, bg_ref, lam_ref, u_ref,
                xp_scr, xc_scr, h_scr, *, n_lat, rc):
    t, cw = xr_ref.shape
    n_ctx = t - n_lat
    pad = SUBLANES
    zpad = jnp.zeros((pad, cw), F32)
    xp_scr[0:pad, :] = zpad
    xp_scr[pad:pad + n_lat, :] = xr_ref[0:n_lat, :]
    xp_scr[pad + n_lat:2 * pad + n_lat, :] = zpad
    xp_scr[2 * pad + n_lat:2 * pad + t, :] = xr_ref[n_lat:t, :]
    xp_scr[2 * pad + t:3 * pad + t, :] = zpad
    for (src0, dst0, n) in ((pad, 0, n_lat), (2 * pad + n_lat, n_lat, n_ctx)):
        acc = cb_ref[...] + cw_ref[0:1, :] * xp_scr[src0 - 2:src0 - 2 + n, :]
        for k in range(1, CONV_W):
            acc = acc + cw_ref[k:k + 1, :] * xp_scr[src0 - 2 + k:src0 - 2 + k + n, :]
        xc_scr[dst0:dst0 + n, :] = acc
    lam = lam_ref[...]
    sp = jnp.maximum(-lam, 0.0) + jnp.log(1.0 + jnp.exp(-jnp.abs(lam)))
    nchunks = t // rc
    ngroups = rc // SUBLANES

    def chunk(c, carry, d, reverse):
        base = pl.multiple_of(c * rc, rc)
        xc = xc_scr[pl.ds(base, rc), :]
        xb = xc.astype(BF16)
        a_parts, b_parts = [], []
        for kb in range(cw // LRU_BW):
            lo = kb * LRU_BW
            g = jnp.dot(xb[:, lo:lo + LRU_BW], wg_ref[d, kb], preferred_element_type=F32)
            g = g + bg_ref[d, kb]
            r = _sigmoid(g[:, :LRU_BW])
            ig = _sigmoid(g[:, LRU_BW:])
            a = jnp.exp((-LRU_C * sp[d:d + 1, lo:lo + LRU_BW]) * r)
            a_parts.append(a)
            b_parts.append(jnp.sqrt(1.0 - a * a) * (ig * xc[:, lo:lo + LRU_BW]))
        a = jnp.concatenate(a_parts, axis=1)
        bt = jnp.concatenate(b_parts, axis=1)
        a, bt = _group_scan(a, bt, reverse)
        a3 = a.reshape(ngroups, SUBLANES, cw)
        b3 = bt.reshape(ngroups, SUBLANES, cw)
        outs = [None] * ngroups
        order = range(ngroups - 1, -1, -1) if reverse else range(ngroups)
        for gidx in order:
            hg = b3[gidx] + a3[gidx] * carry
            outs[gidx] = hg
            carry = hg[0:1, :] if reverse else hg[SUBLANES - 1:SUBLANES, :]
        hs = jnp.concatenate(outs, axis=0)
        if reverse:
            h_scr[pl.ds(base, rc), :] = h_scr[pl.ds(base, rc), :] + hs
        else:
            h_scr[pl.ds(base, rc), :] = hs
        return carry

    n_lat_chunks = n_lat // rc
    carry = jnp.zeros((1, cw), F32)
    carry = lax.fori_loop(n_lat_chunks, nchunks, lambda c, cr: chunk(c, cr, 0, False), carry)
    carry = lax.fori_loop(0, n_lat_chunks, lambda c, cr: chunk(c, cr, 0, False), carry)
    carry = jnp.zeros((1, cw), F32)
    lax.fori_loop(0, nchunks, lambda j, cr: chunk(nchunks - 1 - j, cr, 1, True), carry)
    u_ref[...] = (gy_ref[...].astype(F32) * h_scr[0:n_lat, :]).astype(BF16)


def _lru_core(xr, gy, conv_w, conv_b, w_gates, b_gates, lam, *, n_lat, cw, rc):
    bsz, t, d = xr.shape
    nb = cw // LRU_BW
    kern = functools.partial(_lru_kernel, n_lat=n_lat, rc=rc)
    return pl.pallas_call(
        kern,
        grid=(bsz, d // cw),
        in_specs=[pl.BlockSpec((None, t, cw), lambda b, k: (b, 0, k)),
                  pl.BlockSpec((None, n_lat, cw), lambda b, k: (b, 0, k)),
                  pl.BlockSpec((CONV_W, cw), lambda b, k: (0, k)),
                  pl.BlockSpec((1, cw), lambda b, k: (0, k)),
                  pl.BlockSpec((2, nb, LRU_BW, 2 * LRU_BW), lambda b, k: (0, k, 0, 0)),
                  pl.BlockSpec((2, nb, 1, 2 * LRU_BW), lambda b, k: (0, k, 0, 0)),
                  pl.BlockSpec((2, cw), lambda b, k: (0, k))],
        out_specs=pl.BlockSpec((None, n_lat, cw), lambda b, k: (b, 0, k)),
        out_shape=jax.ShapeDtypeStruct((bsz, n_lat, d), BF16),
        scratch_shapes=[pltpu.VMEM((t + 3 * SUBLANES, cw), F32),
                        pltpu.VMEM((t, cw), F32),
                        pltpu.VMEM((t, cw), F32)],
        compiler_params=_cparams(("arbitrary", "arbitrary")),
        name="lru_core",
    )(xr, gy, conv_w, conv_b, w_gates, b_gates, lam)


def _norm_probs_kernel(h_ref, g_ref, sh_ref, sc_ref, wr_ref, m_ref, p_ref, *,
                       n_lat, ctx_row, n_exp):
    b = pl.program_id(0)
    i = pl.program_id(1)
    tt, d = h_ref.shape
    shift = _row_mod(sh_ref, b, ctx_row, i * tt, tt, n_lat)
    scale = _row_mod(sc_ref, b, ctx_row, i * tt, tt, n_lat)
    m = _ln_mod(h_ref[...], g_ref[...], shift, scale)
    m_ref[...] = m.astype(BF16)
    logits = jnp.dot(m, wr_ref[...], preferred_element_type=F32, precision=lax.Precision.HIGHEST)
    lane = lax.broadcasted_iota(I32, logits.shape, 1)
    lg = jnp.where(lane < n_exp, logits, -jnp.inf)
    ex = jnp.exp(lg - jnp.max(lg, axis=1, keepdims=True))
    p_ref[...] = ex / jnp.sum(ex, axis=1, keepdims=True)


def _norm_probs(h, g, mod, wr, *, n_lat, n_exp, tt):
    bsz, t, d = h.shape
    kern = functools.partial(_norm_probs_kernel, n_lat=n_lat, ctx_row=bsz, n_exp=n_exp)
    return pl.pallas_call(
        kern,
        grid=(bsz, t // tt),
        in_specs=[pl.BlockSpec((None, tt, d), lambda b, i: (b, i, 0)),
                  pl.BlockSpec((1, d), lambda b, i: (0, 0)),
                  pl.BlockSpec((MOD_ROWS, d), lambda b, i: (0, 3)),
                  pl.BlockSpec((MOD_ROWS, d), lambda b, i: (0, 4)),
                  pl.BlockSpec((d, LANES), lambda b, i: (0, 0))],
        out_specs=[pl.BlockSpec((None, tt, d), lambda b, i: (b, i, 0)),
                   pl.BlockSpec((None, tt, LANES), lambda b, i: (b, i, 0))],
        out_shape=[jax.ShapeDtypeStruct((bsz, t, d), BF16),
                   jax.ShapeDtypeStruct((bsz, t, LANES), F32)],
        compiler_params=_cparams(("arbitrary", "arbitrary")),
        name="moe_norm_probs",
    )(h, g, mod, mod, wr)


def _select(p, ltri, cap):
    n_tok = p.shape[0]

    def count(mask):
        c = jnp.where(mask, 1.0, 0.0).reshape(n_tok // SUBLANES, SUBLANES, LANES)
        return jnp.sum(jnp.sum(c, axis=0), axis=0, keepdims=True)

    def body(i, thr):
        cand = thr | jnp.left_shift(jnp.int32(1), 29 - i)
        return jnp.where(count(p >= pltpu.bitcast(cand, F32)) >= cap, cand, thr)

    thr = lax.fori_loop(0, 30, body, jnp.zeros((1, LANES), I32))
    gt = p >= pltpu.bitcast(thr + 1, F32)
    eq = (p >= pltpu.bitcast(thr, F32)) & jnp.logical_not(gt)
    need = cap - count(gt)
    pre_eq = jnp.dot(ltri, jnp.where(eq, 1.0, 0.0).astype(BF16), preferred_element_type=F32)
    sel = gt | (eq & (pre_eq < need))
    slot = jnp.dot(ltri, jnp.where(sel, 1.0, 0.0).astype(BF16), preferred_element_type=F32)
    return jnp.where(sel, slot, -1.0), jnp.where(sel, p, 0.0)


def _select_kernel(p_ref, slot_ref, gate_ref, slot_t_ref, ltri_scr, *, n_lat, n_exp):
    b = pl.program_id(0)
    t = p_ref.shape[0]
    n_ctx = t - n_lat

    @pl.when(b == 0)
    def _():
        r = lax.broadcasted_iota(I32, (n_lat, n_lat), 0)
        c = lax.broadcasted_iota(I32, (n_lat, n_lat), 1)
        ltri_scr[...] = jnp.where(c < r, 1.0, 0.0).astype(BF16)

    slot, gate = _select(p_ref[0:n_lat, :], ltri_scr[...], EC_CAPACITY * n_lat // n_exp)
    slot_ref[0:n_lat, :] = slot
    gate_ref[0:n_lat, :] = gate
    slot_t_ref[:, 0:n_lat] = slot.T
    if n_ctx:
        slot, gate = _select(p_ref[n_lat:t, :], ltri_scr[0:n_ctx, 0:n_ctx],
                             EC_CAPACITY * n_ctx // n_exp)
        slot_ref[n_lat:t, :] = slot
        gate_ref[n_lat:t, :] = gate
        slot_t_ref[:, n_lat:t] = slot.T


def _select_tokens(p, *, n_lat, n_exp):
    bsz, t, _ = p.shape
    kern = functools.partial(_select_kernel, n_lat=n_lat, n_exp=n_exp)
    return pl.pallas_call(
        kern,
        grid=(bsz,),
        in_specs=[pl.BlockSpec((None, t, LANES), lambda b: (b, 0, 0))],
        out_specs=[pl.BlockSpec((None, t, LANES), lambda b: (b, 0, 0)),
                   pl.BlockSpec((None, t, LANES), lambda b: (b, 0, 0)),
                   pl.BlockSpec((None, LANES, t), lambda b: (b, 0, 0))],
        out_shape=[jax.ShapeDtypeStruct((bsz, t, LANES), F32),
                   jax.ShapeDtypeStruct((bsz, t, LANES), F32),
                   jax.ShapeDtypeStruct((bsz, LANES, t), F32)],
        scratch_shapes=[pltpu.VMEM((n_lat, n_lat), BF16)],
        compiler_params=_cparams(("arbitrary",)),
        name="moe_select",
    )(p)


def _gather_kernel(m_ref, slot_t_ref, xg_ref, *, n_lat, cap_lat, cap_ctx):
    j = pl.program_id(1)
    eg = xg_ref.shape[0]
    t = m_ref.shape[0]
    for ee in range(eg):
        row = slot_t_ref[pl.ds(j * eg + ee, 1), :]
        sl = lax.broadcasted_iota(I32, (cap_lat, 1), 0).astype(F32)
        p = jnp.where(row[:, 0:n_lat] == sl, 1.0, 0.0).astype(BF16)
        xg_ref[ee, 0:cap_lat, :] = jnp.dot(p, m_ref[0:n_lat, :],
                                           preferred_element_type=F32).astype(BF16)
        if cap_ctx:
            sc = lax.broadcasted_iota(I32, (cap_ctx, 1), 0).astype(F32)
            pc = jnp.where(row[:, n_lat:t] == sc, 1.0, 0.0).astype(BF16)
            xg_ref[ee, cap_lat:cap_lat + cap_ctx, :] = jnp.dot(
                pc, m_ref[n_lat:t, :], preferred_element_type=F32).astype(BF16)


def _gather(m, slot_t, *, n_lat, n_exp, eg):
    bsz, t, d = m.shape
    cap_lat = EC_CAPACITY * n_lat // n_exp
    cap_ctx = EC_CAPACITY * (t - n_lat) // n_exp
    r = cap_lat + cap_ctx
    kern = functools.partial(_gather_kernel, n_lat=n_lat, cap_lat=cap_lat, cap_ctx=cap_ctx)
    return pl.pallas_call(
        kern,
        grid=(bsz, n_exp // eg),
        in_specs=[pl.BlockSpec((None, t, d), lambda b, j: (b, 0, 0)),
                  pl.BlockSpec((None, LANES, t), lambda b, j: (b, 0, 0))],
        out_specs=pl.BlockSpec((eg, None, r, d), lambda b, j: (j, b, 0, 0)),
        out_shape=jax.ShapeDtypeStruct((n_exp, bsz, r, d), BF16),
        compiler_params=_cparams(("arbitrary", "arbitrary")),
        name="moe_gather",
    )(m, slot_t)


def _ffn_kernel(x_ref, wg_ref, wu_ref, wd_ref, y_ref, acc_scr, *, rc):
    j = pl.program_id(1)
    nj = pl.num_programs(1)
    rows = x_ref.shape[0]

    @pl.when(j == 0)
    def _():
        acc_scr[...] = jnp.zeros_like(acc_scr)

    wg = wg_ref[...].astype(BF16)
    wu = wu_ref[...].astype(BF16)
    wd = wd_ref[...].astype(BF16)

    for c in range(rows // rc):
        x = x_ref[c * rc:(c + 1) * rc, :]
        g = jnp.dot(x, wg, preferred_element_type=F32)
        u = jnp.dot(x, wu, preferred_element_type=F32)
        hid = ((g * _sigmoid(g)) * u).astype(BF16)
        acc_scr[c * rc:(c + 1) * rc, :] += jnp.dot(hid, wd, preferred_element_type=F32)

    @pl.when(j == nj - 1)
    def _():
        y_ref[...] = acc_scr[...].astype(BF16)


def _ffn(xg, w_gate_up, w_down, *, layer, fc, rc):
    n_exp, rows, d = xg.shape
    f = w_down.shape[2]
    nj = f // fc
    kern = functools.partial(_ffn_kernel, rc=rc)
    return pl.pallas_call(
        kern,
        grid=(n_exp, nj),
        in_specs=[pl.BlockSpec((None, rows, d), lambda e, j: (e, 0, 0)),
                  pl.BlockSpec((None, None, d, fc), lambda e, j: (layer, e, 0, j)),
                  pl.BlockSpec((None, None, d, fc), lambda e, j: (layer, e, 0, nj + j)),
                  pl.BlockSpec((None, None, fc, d), lambda e, j: (layer, e, j, 0))],
        out_specs=pl.BlockSpec((None, rows, d), lambda e, j: (e, 0, 0)),
        out_shape=jax.ShapeDtypeStruct((n_exp, rows, d), BF16),
        scratch_shapes=[pltpu.VMEM((rows, d), F32)],
        compiler_params=_cparams(("arbitrary", "arbitrary")),
        name="moe_ffn",
    )(xg, w_gate_up, w_gate_up, w_down)


def _combine_kernel(y_ref, slot_ref, gate_ref, h_ref, g2_ref, fg_ref, o_ref, *,
                    n_lat, ctx_row, cap_lat, cap_ctx, final_norm):
    b = pl.program_id(0)
    i = pl.program_id(1)
    tt, d = h_ref.shape
    n_exp = y_ref.shape[0]
    n_lat_tiles = n_lat // tt

    def finish(acc):
        gate2 = _row_mod(g2_ref, b, ctx_row, i * tt, tt, n_lat)
        out = h_ref[...] + gate2 * acc
        if final_norm:
            ms = jnp.mean(out * out, axis=-1, keepdims=True)
            out = (out * lax.rsqrt(ms + EPS)) * fg_ref[...]
        o_ref[...] = out

    @pl.when(i < n_lat_tiles)
    def _():
        slot = slot_ref[...]
        gate = gate_ref[...]
        iota_c = lax.broadcasted_iota(I32, (1, cap_lat), 1).astype(F32)
        acc = jnp.zeros((tt, d), F32)
        for e in range(n_exp):
            pt = jnp.where(slot[:, e:e + 1] == iota_c, gate[:, e:e + 1], 0.0).astype(BF16)
            acc = acc + jnp.dot(pt, y_ref[e, 0:cap_lat, :], preferred_element_type=F32)
        finish(acc)

    if cap_ctx:
        @pl.when(i >= n_lat_tiles)
        def _():
            slot = slot_ref[...]
            gate = gate_ref[...]
            iota_c = lax.broadcasted_iota(I32, (1, n_exp * cap_ctx), 1).astype(F32)
            pt = jnp.zeros((tt, n_exp * cap_ctx), F32)
            for e in range(n_exp):
                pt = pt + jnp.where(slot[:, e:e + 1] + float(e * cap_ctx) == iota_c,
                                    gate[:, e:e + 1], 0.0)
            yc = y_ref[:, cap_lat:cap_lat + cap_ctx, :].reshape(n_exp * cap_ctx, d)
            finish(jnp.dot(pt.astype(BF16), yc, preferred_element_type=F32))


def _combine(y, slot, gate, h, mod, final_g, *, n_lat, n_out, tt, final_norm):
    n_exp, bsz, r, d = y.shape
    t = slot.shape[1]
    cap_lat = EC_CAPACITY * n_lat // n_exp
    cap_ctx = EC_CAPACITY * (t - n_lat) // n_exp
    kern = functools.partial(_combine_kernel, n_lat=n_lat, ctx_row=bsz, cap_lat=cap_lat,
                             cap_ctx=cap_ctx, final_norm=final_norm)
    return pl.pallas_call(
        kern,
        grid=(bsz, n_out // tt),
        in_specs=[pl.BlockSpec((n_exp, None, r, d), lambda b, i: (0, b, 0, 0)),
                  pl.BlockSpec((None, tt, LANES), lambda b, i: (b, i, 0)),
                  pl.BlockSpec((None, tt, LANES), lambda b, i: (b, i, 0)),
                  pl.BlockSpec((None, tt, d), lambda b, i: (b, i, 0)),
                  pl.BlockSpec((MOD_ROWS, d), lambda b, i: (0, 5)),
                  pl.BlockSpec((1, d), lambda b, i: (0, 0))],
        out_specs=pl.BlockSpec((None, tt, d), lambda b, i: (b, i, 0)),
        out_shape=jax.ShapeDtypeStruct((bsz, n_out, d), F32),
        compiler_params=_cparams(("arbitrary", "arbitrary")),
        name="moe_combine",
    )(y, slot, gate, h, mod, final_g)


def _moe(h, g2, mod, w_router, w_gate_up, w_down, final_g, *, layer, n_lat, final_norm):
    bsz, t, d = h.shape
    n_exp = w_router.shape[-1]
    wr = jnp.pad(w_router, ((0, 0), (0, LANES - n_exp)))
    m, p = _norm_probs(h, g2, mod, wr, n_lat=n_lat, n_exp=n_exp, tt=_pick(t, (768, 512, 256)))
    slot, gate, slot_t = _select_tokens(p, n_lat=n_lat, n_exp=n_exp)
    xg = _gather(m, slot_t, n_lat=n_lat, n_exp=n_exp, eg=4)
    r = xg.shape[2]
    y = _ffn(xg.reshape(n_exp, bsz * r, d), w_gate_up, w_down, layer=layer, fc=256,
             rc=_pick(bsz * r, (768, 1024, 512, 256, 128, 64, 32, 16)))
    return _combine(y.reshape(n_exp, bsz, r, d), slot, gate, h, mod, final_g,
                    n_lat=n_lat, n_out=t, tt=256, final_norm=final_norm)


def kernel(x, c, ctx, c_ctx, ada_w, ada_b, norm1_g, norm2_g, final_g, attn_w_qkv, attn_lq1, attn_lk1, attn_lq2, attn_lk2, attn_subln_g, attn_w_o, lru_w_in, lru_b_in, lru_conv_w, lru_conv_b, lru_w_gates, lru_b_gates, lru_lambda, lru_w_out, moe_w_router, moe_w_gate_up, moe_w_down):
    bsz, n_lat, d = x.shape
    n_ctx = ctx.shape[1]
    depth = ada_w.shape[0]
    assert bsz < MOD_ROWS and d % (2 * HEAD_W) == 0 and n_lat % GRID_W == 0

    cc = jnp.concatenate([c, c_ctx[None, :], jnp.zeros((MOD_ROWS - bsz - 1, d), F32)], axis=0)
    mods = _mod_tables(cc, ada_w, ada_b)
    h = jnp.concatenate([x, ctx], axis=1)
    row = lambda v: v.reshape(1, -1)

    for i in range(depth):
        last = i == depth - 1
        mod = mods[i]
        j = i // N_MIXERS
        if i % N_MIXERS == 0:
            lam_init = 0.8 - 0.6 * math.exp(-0.3 * i)
            w = attn_w_qkv[j]
            wqk = w[:, :2 * d].astype(BF16)
            wvt = w[:, 2 * d:].T.astype(BF16)
            cos, s1, s2 = _rope_tables(n_lat, n_ctx)
            t_all = n_lat + n_ctx
            qk, vt = _qkv(h, row(norm1_g[i]), mod, wqk, wvt, cos, s1, s2, n_lat=n_lat,
                          tt=_pick(t_all, (768, 256)))
            lamv = jnp.stack([attn_lq1[j], attn_lk1[j], attn_lq2[j], attn_lk2[j]], axis=0)
            sg = attn_subln_g[j].reshape(HEAD_W, 1)
            o = _attention(qk, vt, lamv, sg, None, q_row0=0, n_q=n_lat, k_row0=0,
                           n_k=n_lat + n_ctx, tq=256, kc=256, nh=2, lam_init=lam_init)
            n_rows = n_lat
            if not last:
                o = _attention(qk, vt, lamv, sg, o, q_row0=n_lat, n_q=n_ctx, k_row0=n_lat,
                               n_k=n_ctx, tq=n_ctx, kc=n_ctx, nh=2, lam_init=lam_init)
                n_rows = n_lat + n_ctx
            h = _proj_res(o, attn_w_o[j].astype(BF16), h, mod, gate_chunk=2, n_rows=n_rows,
                          n_lat=n_lat, tt=_pick(n_rows, (768, 512, 256)))
        else:
            gy, xr = _lru_in(h, row(norm1_g[i]), mod, lru_w_in[j].astype(BF16),
                             row(lru_b_in[j]), n_lat=n_lat, tt=_pick(n_lat + n_ctx, (768, 256)))
            nblk = d // LRU_BW
            u = _lru_core(xr, gy, lru_conv_w[j], row(lru_conv_b[j]),
                          lru_w_gates[j].astype(BF16),
                          lru_b_gates[j].reshape(2, nblk, 1, 2 * LRU_BW), lru_lambda[j],
                          n_lat=n_lat, cw=256, rc=256)
            n_rows = n_lat if last else n_lat + n_ctx
            assert last, "context output of the recurrent mixer is only needed in non-final layers"
            h = _proj_res(u, lru_w_out[j].astype(BF16), h, mod, gate_chunk=2, n_rows=n_rows,
                          n_lat=n_lat, tt=_pick(n_rows, (1024, 512, 256)))
        h = _moe(h, row(norm2_g[i]), mod, moe_w_router[i], moe_w_gate_up, moe_w_down,
                 row(final_g), layer=i, n_lat=n_lat, final_norm=last)
    return h
```

```python
import functools
import math

import jax
import jax.numpy as jnp
from jax import lax
from jax.experimental import pallas as pl
from jax.experimental.pallas import tpu as pltpu

F32 = jnp.float32
BF16 = jnp.bfloat16
I32 = jnp.int32

EPS = 1e-6
GRID_W = 64
ROPE_THETA = 10000.0
LRU_C = 8.0
EC_CAPACITY = 2
N_MIXERS = 2
HEAD_W = 128
LRU_BW = 128
LRU_SEG = 32
LRU_PITCH = 40
CONV_W = 4
MOD_ROWS = 16
PREFIX_BLOCK = 256
LANES = 128
SUBLANES = 8
VMEM_LIMIT = 56 * 2**20


def _pick(n, cands):
    return next(c for c in cands if n % c == 0)


def _cparams(sem, flags=None):
    return pltpu.CompilerParams(dimension_semantics=sem, vmem_limit_bytes=VMEM_LIMIT, flags=flags)


def _sigmoid(x):
    return 1.0 / (1.0 + jnp.exp(-x))


def _gelu_tanh(x):
    return 0.5 * x * (1.0 + jnp.tanh(math.sqrt(2.0 / math.pi) * (x + 0.044715 * (x * x * x))))


def _row_mod(mod_ref, b, ctx_row, row0, nrows, n_lat):
    mb = mod_ref[pl.ds(b, 1), :]
    mc = mod_ref[ctx_row:ctx_row + 1, :]
    rows = row0 + lax.broadcasted_iota(I32, (nrows, 1), 0)
    return jnp.where(rows < n_lat, mb, mc)


def _ln_mod(x, g, shift, scale):
    ms = jnp.mean(x * x, axis=-1, keepdims=True)
    y = (x * lax.rsqrt(ms + EPS)) * g
    return y * (1.0 + scale) + shift


def _mod_kernel(c_ref, w_ref, b_ref, o_ref):
    c = c_ref[...]
    s = c * _sigmoid(c)
    o_ref[...] = jnp.dot(s, w_ref[...], preferred_element_type=F32,
                         precision=lax.Precision.HIGHEST) + b_ref[...]


def _mod_tables(cc, ada_w, ada_b):
    depth, d, n6 = ada_w.shape
    tn = _pick(n6, (1536, 768, 384))
    return pl.pallas_call(
        _mod_kernel,
        grid=(depth, n6 // tn),
        in_specs=[pl.BlockSpec((MOD_ROWS, d), lambda i, j: (0, 0)),
                  pl.BlockSpec((None, d, tn), lambda i, j: (i, 0, j)),
                  pl.BlockSpec((None, 1, tn), lambda i, j: (i, 0, j))],
        out_specs=pl.BlockSpec((None, MOD_ROWS, tn), lambda i, j: (i, 0, j)),
        out_shape=jax.ShapeDtypeStruct((depth, MOD_ROWS, n6), F32),
        compiler_params=_cparams(("arbitrary", "arbitrary")),
        name="mod_tables",
    )(cc, ada_w, ada_b.reshape(depth, 1, n6))


def _qkv_kernel(h_ref, g_ref, sh_ref, sc_ref, wqk_ref, wvt_ref, cos_ref, s1_ref, s2_ref,
                qk_ref, vt_ref, *, n_lat, ctx_row):
    b = pl.program_id(0)
    i = pl.program_id(1)
    tt, d = h_ref.shape
    shift = _row_mod(sh_ref, b, ctx_row, i * tt, tt, n_lat)
    scale = _row_mod(sc_ref, b, ctx_row, i * tt, tt, n_lat)
    xn = _ln_mod(h_ref[...], g_ref[...], shift, scale).astype(BF16)
    cw = 2 * HEAD_W
    cos = jnp.concatenate([cos_ref[...]] * 2, axis=1)
    s1 = jnp.concatenate([s1_ref[...]] * 2, axis=1)
    s2 = jnp.concatenate([s2_ref[...]] * 2, axis=1)
    for n in range(2 * d // cw):
        acc = jnp.dot(xn, wqk_ref[:, n * cw:(n + 1) * cw], preferred_element_type=F32)
        r = acc * cos + pltpu.roll(acc, cw - 16, 1) * s1 + pltpu.roll(acc, 16, 1) * s2
        if n * cw < d:
            r = r * ((0.5 * HEAD_W) ** -0.5 * math.log2(math.e))
        qk_ref[:, n * cw:(n + 1) * cw] = r.astype(BF16)
    for n in range(d // cw):
        vt = lax.dot_general(wvt_ref[n * cw:(n + 1) * cw, :], xn, (((1,), (1,)), ((), ())),
                             preferred_element_type=F32)
        vt_ref[n * cw:(n + 1) * cw, :] = vt.astype(BF16)


def _qkv(h, g, mod, wqk, wvt, cos, s1, s2, *, n_lat, tt):
    bsz, t, d = h.shape
    kern = functools.partial(_qkv_kernel, n_lat=n_lat, ctx_row=bsz)
    return pl.pallas_call(
        kern,
        grid=(bsz, t // tt),
        in_specs=[pl.BlockSpec((None, tt, d), lambda b, i: (b, i, 0)),
                  pl.BlockSpec((1, d), lambda b, i: (0, 0)),
                  pl.BlockSpec((MOD_ROWS, d), lambda b, i: (0, 0)),
                  pl.BlockSpec((MOD_ROWS, d), lambda b, i: (0, 1)),
                  pl.BlockSpec((d, 2 * d), lambda b, i: (0, 0)),
                  pl.BlockSpec((d, d), lambda b, i: (0, 0)),
                  pl.BlockSpec((tt, HEAD_W), lambda b, i: (i, 0)),
                  pl.BlockSpec((tt, HEAD_W), lambda b, i: (i, 0)),
                  pl.BlockSpec((tt, HEAD_W), lambda b, i: (i, 0))],
        out_specs=[pl.BlockSpec((None, tt, 2 * d), lambda b, i: (b, i, 0)),
                   pl.BlockSpec((None, d, tt), lambda b, i: (b, 0, i))],
        out_shape=[jax.ShapeDtypeStruct((bsz, t, 2 * d), BF16),
                   jax.ShapeDtypeStruct((bsz, d, t), BF16)],
        compiler_params=_cparams(("arbitrary", "arbitrary")),
        name="qkv_rope",
    )(h, g, mod, mod, wqk, wvt, cos, s1, s2)


def _rope_tables(n_lat, n_ctx):
    freqs = HEAD_W // 8
    rows = n_lat // GRID_W
    row = jnp.repeat(jnp.arange(rows), GRID_W).astype(F32)
    col = jnp.tile(jnp.arange(GRID_W), rows).astype(F32)
    inv = ROPE_THETA ** (-(jnp.arange(freqs, dtype=F32) * 2.0) / (2 * freqs))
    lane = jnp.arange(HEAD_W)
    dd = lane % (HEAD_W // 2)
    axis = dd // (2 * freqs)
    half = (dd % (2 * freqs)) // freqs
    f = dd % freqs
    pos = jnp.where(axis[None, :] == 0, row[:, None], col[:, None])
    ang = pos * inv[f][None, :]
    cos = jnp.cos(ang)
    sin = jnp.sin(ang)
    s1 = jnp.where(half[None, :] == 0, -sin, 0.0)
    s2 = jnp.where(half[None, :] == 1, sin, 0.0)
    pad = lambda a, v: jnp.concatenate([a, jnp.full((n_ctx, HEAD_W), v, F32)], axis=0)
    return pad(cos, 1.0), pad(s1, 0.0), pad(s2, 0.0)


def _attn_kernel(lam_ref, sg_ref, q_ref, k_ref, vt_ref, o_ref, s_scr, *, kc, lam_init):
    tq = q_ref.shape[0]
    tk = k_ref.shape[0]
    nh = q_ref.shape[1] // HEAD_W
    nchunks = tk // kc
    lv = lam_ref[...]
    lam = (jnp.exp(jnp.sum(lv[0:1, :] * lv[1:2, :], axis=1, keepdims=True))
           - jnp.exp(jnp.sum(lv[2:3, :] * lv[3:4, :], axis=1, keepdims=True)) + lam_init)
    lane = lax.broadcasted_iota(I32, (tq, HEAD_W), 1)
    ms = []
    for hh in range(nh):
        q = q_ref[:, hh * HEAD_W:(hh + 1) * HEAD_W].astype(F32)
        qbd = jnp.concatenate([jnp.where(lane < HEAD_W // 2, q, 0.0),
                               jnp.where(lane >= HEAD_W // 2, q, 0.0)], axis=0).astype(BF16)
        m8 = jnp.full((SUBLANES, 2 * tq), -jnp.inf, F32)
        for c in range(nchunks):
            s = lax.dot_general(k_ref[c * kc:(c + 1) * kc, hh * HEAD_W:(hh + 1) * HEAD_W], qbd,
                                (((1,), (1,)), ((), ())), preferred_element_type=F32)
            s_scr[hh, c * kc:(c + 1) * kc, :] = s
            m8 = jnp.maximum(m8, jnp.max(s.reshape(kc // SUBLANES, SUBLANES, 2 * tq), axis=0))
        ms.append(jnp.max(m8, axis=0, keepdims=True))
    for hh in range(nh):
        l8 = jnp.zeros((SUBLANES, 2 * tq), F32)
        acc = jnp.zeros((HEAD_W, 2 * tq), F32)
        for c in range(nchunks):
            e = jnp.exp2(s_scr[hh, c * kc:(c + 1) * kc, :] - ms[hh])
            l8 = l8 + jnp.sum(e.reshape(kc // SUBLANES, SUBLANES, 2 * tq), axis=0)
            acc = acc + jnp.dot(vt_ref[hh * HEAD_W:(hh + 1) * HEAD_W, c * kc:(c + 1) * kc],
                                e.astype(BF16), preferred_element_type=F32)
        r = 1.0 / jnp.sum(l8, axis=0, keepdims=True)
        ot = acc[:, :tq] * r[:, :tq] - lam * (acc[:, tq:] * r[:, tq:])
        msq = jnp.mean(ot * ot, axis=0, keepdims=True)
        ot = (ot * lax.rsqrt(msq + EPS)) * sg_ref[...] * (1.0 - lam_init)
        o_ref[:, hh * HEAD_W:(hh + 1) * HEAD_W] = ot.T.astype(BF16)


def _attention(qk, vt, lamv, sg, o_prev, *, q_row0, n_q, k_row0, n_k, tq, kc, nh, lam_init):
    bsz, t, d2 = qk.shape
    d = d2 // 2
    hw = nh * HEAD_W
    heads = d // hw
    qb0 = q_row0 // tq
    kb0 = k_row0 // n_k
    kern = functools.partial(_attn_kernel, kc=kc, lam_init=lam_init)
    in_specs = [pl.BlockSpec((4, HEAD_W // 2), lambda b, h, i: (0, 0)),
                pl.BlockSpec((HEAD_W, 1), lambda b, h, i: (0, 0)),
                pl.BlockSpec((None, tq, hw), lambda b, h, i: (b, qb0 + i, h)),
                pl.BlockSpec((None, n_k, hw), lambda b, h, i: (b, kb0, heads + h)),
                pl.BlockSpec((None, hw, n_k), lambda b, h, i: (b, h, kb0))]
    args = [lamv, sg, qk, qk, vt]
    aliases = {}
    if o_prev is not None:
        in_specs.append(pl.BlockSpec(memory_space=pl.ANY))
        args.append(o_prev)
        aliases = {5: 0}
        body = lambda a, s_, q_, k_, v_, prev_, o_, scr_: kern(a, s_, q_, k_, v_, o_, scr_)
    else:
        body = kern
    return pl.pallas_call(
        body,
        grid=(bsz, heads, n_q // tq),
        in_specs=in_specs,
        out_specs=pl.BlockSpec((None, tq, hw), lambda b, h, i: (b, qb0 + i, h)),
        out_shape=jax.ShapeDtypeStruct((bsz, t, d), BF16),
        scratch_shapes=[pltpu.VMEM((nh, n_k, 2 * tq), F32)],
        input_output_aliases=aliases,
        compiler_params=_cparams(("arbitrary", "arbitrary", "arbitrary")),
        name="diff_attn_ctx" if o_prev is not None else "diff_attn_lat",
    )(*args)


def _proj_res_kernel(u_ref, w_ref, h_ref, gate_ref, o_ref, *, n_lat, ctx_row, nc):
    b = pl.program_id(0)
    i = pl.program_id(1)
    tt, d = h_ref.shape
    gate = _row_mod(gate_ref, b, ctx_row, i * tt, tt, n_lat)
    u = u_ref[...]
    for n in range(d // nc):
        y = jnp.dot(u, w_ref[:, n * nc:(n + 1) * nc], preferred_element_type=F32)
        o_ref[:, n * nc:(n + 1) * nc] = (h_ref[:, n * nc:(n + 1) * nc]
                                         + gate[:, n * nc:(n + 1) * nc] * y)


def _proj_res(u, w, h, mod, *, gate_chunk, n_rows, n_lat, tt):
    bsz, _, k = u.shape
    d = h.shape[-1]
    kern = functools.partial(_proj_res_kernel, n_lat=n_lat, ctx_row=bsz, nc=256)
    return pl.pallas_call(
        kern,
        grid=(bsz, n_rows // tt),
        in_specs=[pl.BlockSpec((None, tt, k), lambda b, i: (b, i, 0)),
                  pl.BlockSpec((k, d), lambda b, i: (0, 0)),
                  pl.BlockSpec((None, tt, d), lambda b, i: (b, i, 0)),
                  pl.BlockSpec((MOD_ROWS, d), lambda b, i: (0, gate_chunk))],
        out_specs=pl.BlockSpec((None, tt, d), lambda b, i: (b, i, 0)),
        out_shape=jax.ShapeDtypeStruct((bsz, n_rows, d), F32),
        compiler_params=_cparams(("arbitrary", "arbitrary")),
        name="proj_residual",
    )(u, w, h, mod)


def _lru_in_kernel(h_ref, g_ref, sh_ref, sc_ref, w_ref, bias_ref, gy_ref, xr_ref, *,
                   n_lat, ctx_row, nc):
    b = pl.program_id(0)
    i = pl.program_id(1)
    tt, d = h_ref.shape
    shift = _row_mod(sh_ref, b, ctx_row, i * tt, tt, n_lat)
    scale = _row_mod(sc_ref, b, ctx_row, i * tt, tt, n_lat)
    xn = _ln_mod(h_ref[...], g_ref[...], shift, scale).astype(BF16)
    for n in range(d // nc):
        y = jnp.dot(xn, w_ref[:, n * nc:(n + 1) * nc], preferred_element_type=F32)
        y = y + bias_ref[:, n * nc:(n + 1) * nc]
        gy_ref[:, n * nc:(n + 1) * nc] = _gelu_tanh(y).astype(BF16)
    for n in range(d // nc):
        x = jnp.dot(xn, w_ref[:, d + n * nc:d + (n + 1) * nc], preferred_element_type=F32)
        xr_ref[:, n * nc:(n + 1) * nc] = x + bias_ref[:, d + n * nc:d + (n + 1) * nc]


def _lru_in(h, g, mod, w_in, b_in, *, n_lat, tt):
    bsz, t, d = h.shape
    kern = functools.partial(_lru_in_kernel, n_lat=n_lat, ctx_row=bsz, nc=256)
    return pl.pallas_call(
        kern,
        grid=(bsz, t // tt),
        in_specs=[pl.BlockSpec((None, tt, d), lambda b, i: (b, i, 0)),
                  pl.BlockSpec((1, d), lambda b, i: (0, 0)),
                  pl.BlockSpec((MOD_ROWS, d), lambda b, i: (0, 0)),
                  pl.BlockSpec((MOD_ROWS, d), lambda b, i: (0, 1)),
                  pl.BlockSpec((d, 2 * d), lambda b, i: (0, 0)),
                  pl.BlockSpec((1, 2 * d), lambda b, i: (0, 0))],
        out_specs=[pl.BlockSpec((None, tt, d), lambda b, i: (b, i, 0)),
                   pl.BlockSpec((None, tt, d), lambda b, i: (b, i, 0))],
        out_shape=[jax.ShapeDtypeStruct((bsz, t, d), BF16),
                   jax.ShapeDtypeStruct((bsz, t, d), F32)],
        compiler_params=_cparams(("arbitrary", "arbitrary")),
        name="lru_in_proj",
    )(h, g, mod, mod, w_in, b_in)


def _lru_kernel(xr_ref, gy_ref, cw_ref, cb_ref, wg_ref, bg_ref, lam_ref, u_ref,
                xp_scr, hf_scr, hb_scr, *, n_lat):
    t, cw = xr_ref.shape
    n_ctx = t - n_lat
    seg, pitch, nsub = LRU_SEG, LRU_PITCH, SUBLANES
    rc = seg * nsub
    nseg_lat, nseg_ctx = n_lat // seg, n_ctx // seg
    lat0, ctx0 = 1, nseg_lat + 2
    zseg = jnp.zeros((seg, cw), F32)
    for s in (0, nseg_lat + 1, nseg_lat + nseg_ctx + 2):
        xp_scr[s * pitch:s * pitch + seg, :] = zseg
    for s in range(nseg_lat):
        xp_scr[(lat0 + s) * pitch:(lat0 + s) * pitch + seg, :] = xr_ref[s * seg:(s + 1) * seg, :]
    for s in range(nseg_ctx):
        xp_scr[(ctx0 + s) * pitch:(ctx0 + s) * pitch + seg, :] = (
            xr_ref[n_lat + s * seg:n_lat + (s + 1) * seg, :])
    lam = lam_ref[...]
    sp = jnp.maximum(-lam, 0.0) + jnp.log(1.0 + jnp.exp(-jnp.abs(lam)))
    w = cw_ref[...]
    cb = cb_ref[...]

    def rows8(ref, start):
        return ref[pl.ds(start, nsub, stride=pitch), :]

    def chunk(pb, carry, d, reverse, out_scr):
        x = [rows8(xp_scr, pb + g) for g in range(seg)]
        xm1 = [rows8(xp_scr, pb - (pitch - seg) - 1)] + x[:-1]
        xm2 = [rows8(xp_scr, pb - (pitch - seg) - 2), xm1[0]] + x[:-2]
        xp1 = x[1:] + [rows8(xp_scr, pb + pitch)]
        xc = jnp.concatenate(
            [cb + w[0:1, :] * xm2[g] + w[1:2, :] * xm1[g] + w[2:3, :] * x[g] + w[3:4, :] * xp1[g]
             for g in range(seg)], axis=0)
        gpre = jnp.dot(xc.astype(BF16), wg_ref[d, 0], preferred_element_type=F32) + bg_ref[d, 0]
        r = _sigmoid(gpre[:, :LRU_BW])
        ig = _sigmoid(gpre[:, LRU_BW:])
        a = jnp.exp((-LRU_C * sp[d:d + 1, :]) * r)
        bt = jnp.sqrt(1.0 - a * a) * (ig * xc)
        a3 = a.reshape(seg, nsub, cw)
        b3 = bt.reshape(seg, nsub, cw)
        h = jnp.zeros((nsub, cw), F32)
        p = jnp.ones((nsub, cw), F32)
        hl, pl_ = [None] * seg, [None] * seg
        for g in (range(seg - 1, -1, -1) if reverse else range(seg)):
            h = a3[g] * h + b3[g]
            p = a3[g] * p
            hl[g], pl_[g] = h, p
        entry = [None] * nsub
        for j in (range(nsub - 1, -1, -1) if reverse else range(nsub)):
            entry[j] = carry
            carry = h[j:j + 1, :] + p[j:j + 1, :] * carry
        h_in = jnp.concatenate(entry, axis=0)
        for g in range(seg):
            out_scr[pl.ds(pb + g, nsub, stride=pitch), :] = hl[g] + pl_[g] * h_in
        return carry

    lat_pb = lambda c: lat0 * pitch + c * (nsub * pitch)
    ctx_pb = lambda c: (ctx0 + c * nsub) * pitch
    n_lat_chunks, n_ctx_chunks = n_lat // rc, n_ctx // rc
    cf = cb_ = jnp.zeros((1, cw), F32)
    for c in range(n_ctx_chunks):
        cf = chunk(ctx_pb(c), cf, 0, False, hf_scr)
        cb_ = chunk(ctx_pb(n_ctx_chunks - 1 - c), cb_, 1, True, hb_scr)

    def both(c, carries):
        return (chunk(lat_pb(c), carries[0], 0, False, hf_scr),
                chunk(lat_pb(n_lat_chunks - 1 - c), carries[1], 1, True, hb_scr))

    lax.fori_loop(0, n_lat_chunks, both, (cf, cb_))
    for s in range(nseg_lat):
        rows = slice((lat0 + s) * pitch, (lat0 + s) * pitch + seg)
        u_ref[s * seg:(s + 1) * seg, :] = (gy_ref[s * seg:(s + 1) * seg, :].astype(F32)
                                           * (hf_scr[rows, :] + hb_scr[rows, :])).astype(BF16)


def _lru_core(xr, gy, conv_w, conv_b, w_gates, b_gates, lam, *, n_lat):
    bsz, t, d = xr.shape
    cw = LRU_BW
    rows = ((t // LRU_SEG) + 3) * LRU_PITCH
    kern = functools.partial(_lru_kernel, n_lat=n_lat)
    return pl.pallas_call(
        kern,
        grid=(bsz, d // cw),
        in_specs=[pl.BlockSpec((None, t, cw), lambda b, k: (b, 0, k)),
                  pl.BlockSpec((None, n_lat, cw), lambda b, k: (b, 0, k)),
                  pl.BlockSpec((CONV_W, cw), lambda b, k: (0, k)),
                  pl.BlockSpec((1, cw), lambda b, k: (0, k)),
                  pl.BlockSpec((2, 1, LRU_BW, 2 * LRU_BW), lambda b, k: (0, k, 0, 0)),
                  pl.BlockSpec((2, 1, 1, 2 * LRU_BW), lambda b, k: (0, k, 0, 0)),
                  pl.BlockSpec((2, cw), lambda b, k: (0, k))],
        out_specs=pl.BlockSpec((None, n_lat, cw), lambda b, k: (b, 0, k)),
        out_shape=jax.ShapeDtypeStruct((bsz, n_lat, d), BF16),
        scratch_shapes=[pltpu.VMEM((rows, cw), F32)] * 3,
        compiler_params=_cparams(("arbitrary", "arbitrary")),
        name="lru_core",
    )(xr, gy, conv_w, conv_b, w_gates, b_gates, lam)


def _norm_probs_kernel(h_ref, g_ref, sh_ref, sc_ref, wr_ref, m_ref, p_ref, *,
                       n_lat, ctx_row, n_exp):
    b = pl.program_id(0)
    i = pl.program_id(1)
    tt, d = h_ref.shape
    shift = _row_mod(sh_ref, b, ctx_row, i * tt, tt, n_lat)
    scale = _row_mod(sc_ref, b, ctx_row, i * tt, tt, n_lat)
    m = _ln_mod(h_ref[...], g_ref[...], shift, scale)
    m_hi = m.astype(BF16)
    m_ref[...] = m_hi
    m_lo = (m - m_hi.astype(F32)).astype(BF16)
    wr = wr_ref[...]
    w_hi = wr.astype(BF16)
    w_lo = (wr - w_hi.astype(F32)).astype(BF16)
    logits = (jnp.dot(m_hi, w_hi, preferred_element_type=F32)
              + jnp.dot(m_lo, w_hi, preferred_element_type=F32)
              + jnp.dot(m_hi, w_lo, preferred_element_type=F32))
    lane = lax.broadcasted_iota(I32, logits.shape, 1)
    lg = jnp.where(lane < n_exp, logits, -jnp.inf)
    ex = jnp.exp(lg - jnp.max(lg, axis=1, keepdims=True))
    p_ref[...] = ex / jnp.sum(ex, axis=1, keepdims=True)


def _norm_probs(h, g, mod, wr, *, n_lat, n_exp, tt):
    bsz, t, d = h.shape
    kern = functools.partial(_norm_probs_kernel, n_lat=n_lat, ctx_row=bsz, n_exp=n_exp)
    return pl.pallas_call(
        kern,
        grid=(bsz, t // tt),
        in_specs=[pl.BlockSpec((None, tt, d), lambda b, i: (b, i, 0)),
                  pl.BlockSpec((1, d), lambda b, i: (0, 0)),
                  pl.BlockSpec((MOD_ROWS, d), lambda b, i: (0, 3)),
                  pl.BlockSpec((MOD_ROWS, d), lambda b, i: (0, 4)),
                  pl.BlockSpec((d, LANES), lambda b, i: (0, 0))],
        out_specs=[pl.BlockSpec((None, tt, d), lambda b, i: (b, i, 0)),
                   pl.BlockSpec((None, tt, LANES), lambda b, i: (b, i, 0))],
        out_shape=[jax.ShapeDtypeStruct((bsz, t, d), BF16),
                   jax.ShapeDtypeStruct((bsz, t, LANES), F32)],
        compiler_params=_cparams(("arbitrary", "arbitrary")),
        name="moe_norm_probs",
    )(h, g, mod, mod, wr)


def _select(p, cap, n_exp):
    n_tok = p.shape[0]
    n_rows = -(-n_exp // SUBLANES) * SUBLANES
    pt = p.T[0:n_rows, :]

    def body(i, thr):
        cand = thr | jnp.left_shift(jnp.int32(1), 29 - i)
        hit = jnp.where(pt >= lax.bitcast_convert_type(cand, F32), 1.0, 0.0)
        return jnp.where(jnp.sum(hit, axis=1, keepdims=True) >= cap, cand, thr)

    thr_col = lax.fori_loop(0, 30, body, jnp.zeros((n_rows, 1), I32))
    diag = (lax.broadcasted_iota(I32, (n_rows, LANES), 0)
            == lax.broadcasted_iota(I32, (n_rows, LANES), 1))
    thr = jnp.sum(jnp.where(diag, jnp.broadcast_to(thr_col, (n_rows, LANES)), 0),
                  axis=0, keepdims=True)
    gt = p >= pltpu.bitcast(thr + 1, F32)
    eq = (p >= pltpu.bitcast(thr, F32)) & jnp.logical_not(gt)
    blk = PREFIX_BLOCK
    ltri = jnp.where(lax.broadcasted_iota(I32, (blk, blk), 1)
                     < lax.broadcasted_iota(I32, (blk, blk), 0), 1.0, 0.0).astype(BF16)
    masks = jnp.concatenate([jnp.where(gt, 1.0, 0.0), jnp.where(eq, 1.0, 0.0)], axis=1)
    off = jnp.zeros((1, 2 * LANES), F32)
    pres = []
    for i in range(n_tok // blk):
        mb = masks[i * blk:(i + 1) * blk, :]
        pre = jnp.dot(ltri, mb.astype(BF16), preferred_element_type=F32) + off
        pres.append(pre)
        off = pre[blk - 1:blk, :] + mb[blk - 1:blk, :]
    pre = jnp.concatenate(pres, axis=0)
    pre_gt, pre_eq = pre[:, :LANES], pre[:, LANES:]
    need = cap - off[:, :LANES]
    sel = gt | (eq & (pre_eq < need))
    slot = pre_gt + jnp.minimum(pre_eq, need)
    return jnp.where(sel, slot, -1.0), jnp.where(sel, p, 0.0)


def _select_kernel(p_ref, slot_ref, gate_ref, slot_t_ref, *, n_lat, n_exp):
    t = p_ref.shape[0]
    n_ctx = t - n_lat
    slot, gate = _select(p_ref[0:n_lat, :], EC_CAPACITY * n_lat // n_exp, n_exp)
    slot_ref[0:n_lat, :] = slot
    gate_ref[0:n_lat, :] = gate
    slot_t_ref[:, 0:n_lat] = slot.T
    if n_ctx:
        slot, gate = _select(p_ref[n_lat:t, :], EC_CAPACITY * n_ctx // n_exp, n_exp)
        slot_ref[n_lat:t, :] = slot
        gate_ref[n_lat:t, :] = gate
        slot_t_ref[:, n_lat:t] = slot.T


def _select_tokens(p, *, n_lat, n_exp):
    bsz, t, _ = p.shape
    kern = functools.partial(_select_kernel, n_lat=n_lat, n_exp=n_exp)
    return pl.pallas_call(
        kern,
        grid=(bsz,),
        in_specs=[pl.BlockSpec((None, t, LANES), lambda b: (b, 0, 0))],
        out_specs=[pl.BlockSpec((None, t, LANES), lambda b: (b, 0, 0)),
                   pl.BlockSpec((None, t, LANES), lambda b: (b, 0, 0)),
                   pl.BlockSpec((None, LANES, t), lambda b: (b, 0, 0))],
        out_shape=[jax.ShapeDtypeStruct((bsz, t, LANES), F32),
                   jax.ShapeDtypeStruct((bsz, t, LANES), F32),
                   jax.ShapeDtypeStruct((bsz, LANES, t), F32)],
        compiler_params=_cparams(("arbitrary",)),
        name="moe_select",
    )(p)


def _gather_kernel(m_ref, slot_t_ref, xg_ref, *, n_lat, cap_lat, cap_ctx):
    j = pl.program_id(1)
    eg = xg_ref.shape[0]
    t = m_ref.shape[0]
    for ee in range(eg):
        row = slot_t_ref[pl.ds(j * eg + ee, 1), :]
        sl = lax.broadcasted_iota(I32, (cap_lat, 1), 0).astype(F32)
        p = jnp.where(row[:, 0:n_lat] == sl, 1.0, 0.0).astype(BF16)
        xg_ref[ee, 0:cap_lat, :] = jnp.dot(p, m_ref[0:n_lat, :],
                                           preferred_element_type=F32).astype(BF16)
        if cap_ctx:
            sc = lax.broadcasted_iota(I32, (cap_ctx, 1), 0).astype(F32)
            pc = jnp.where(row[:, n_lat:t] == sc, 1.0, 0.0).astype(BF16)
            xg_ref[ee, cap_lat:cap_lat + cap_ctx, :] = jnp.dot(
                pc, m_ref[n_lat:t, :], preferred_element_type=F32).astype(BF16)


def _gather(m, slot_t, *, n_lat, n_exp, eg):
    bsz, t, d = m.shape
    cap_lat = EC_CAPACITY * n_lat // n_exp
    cap_ctx = EC_CAPACITY * (t - n_lat) // n_exp
    r = cap_lat + cap_ctx
    kern = functools.partial(_gather_kernel, n_lat=n_lat, cap_lat=cap_lat, cap_ctx=cap_ctx)
    return pl.pallas_call(
        kern,
        grid=(bsz, n_exp // eg),
        in_specs=[pl.BlockSpec((None, t, d), lambda b, j: (b, 0, 0)),
                  pl.BlockSpec((None, LANES, t), lambda b, j: (b, 0, 0))],
        out_specs=pl.BlockSpec((eg, None, r, d), lambda b, j: (j, b, 0, 0)),
        out_shape=jax.ShapeDtypeStruct((n_exp, bsz, r, d), BF16),
        compiler_params=_cparams(("arbitrary", "arbitrary")),
        name="moe_gather",
    )(m, slot_t)


def _ffn_kernel(x_ref, wg_ref, wu_ref, wd_ref, y_ref, acc_scr, *, rc):
    j = pl.program_id(1)
    nj = pl.num_programs(1)
    rows = x_ref.shape[0]

    @pl.when(j == 0)
    def _():
        acc_scr[...] = jnp.zeros_like(acc_scr)

    wg = wg_ref[...].astype(BF16)
    wu = wu_ref[...].astype(BF16)
    wd = wd_ref[...].astype(BF16)

    for c in range(rows // rc):
        x = x_ref[c * rc:(c + 1) * rc, :]
        g = jnp.dot(x, wg, preferred_element_type=F32)
        u = jnp.dot(x, wu, preferred_element_type=F32)
        hid = ((g * _sigmoid(g)) * u).astype(BF16)
        acc_scr[c * rc:(c + 1) * rc, :] += jnp.dot(hid, wd, preferred_element_type=F32)

    @pl.when(j == nj - 1)
    def _():
        y_ref[...] = acc_scr[...].astype(BF16)


def _ffn(xg, w_gate_up, w_down, *, layer, fc, rc):
    n_exp, rows, d = xg.shape
    f = w_down.shape[2]
    nj = f // fc
    kern = functools.partial(_ffn_kernel, rc=rc)
    return pl.pallas_call(
        kern,
        grid=(n_exp, nj),
        in_specs=[pl.BlockSpec((None, rows, d), lambda e, j: (e, 0, 0)),
                  pl.BlockSpec((None, None, d, fc), lambda e, j: (layer, e, 0, j)),
                  pl.BlockSpec((None, None, d, fc), lambda e, j: (layer, e, 0, nj + j)),
                  pl.BlockSpec((None, None, fc, d), lambda e, j: (layer, e, j, 0))],
        out_specs=pl.BlockSpec((None, rows, d), lambda e, j: (e, 0, 0)),
        out_shape=jax.ShapeDtypeStruct((n_exp, rows, d), BF16),
        scratch_shapes=[pltpu.VMEM((rows, d), F32)],
        compiler_params=_cparams(("arbitrary", "arbitrary")),
        name="moe_ffn",
    )(xg, w_gate_up, w_gate_up, w_down)


def _combine_kernel(y_ref, slot_ref, gate_ref, h_ref, g2_ref, fg_ref, o_ref, *,
                    n_lat, ctx_row, cap_lat, cap_ctx, final_norm):
    b = pl.program_id(0)
    i = pl.program_id(1)
    tt, d = h_ref.shape
    n_exp = y_ref.shape[0]
    n_lat_tiles = n_lat // tt

    def finish(acc):
        gate2 = _row_mod(g2_ref, b, ctx_row, i * tt, tt, n_lat)
        out = h_ref[...] + gate2 * acc
        if final_norm:
            ms = jnp.mean(out * out, axis=-1, keepdims=True)
            out = (out * lax.rsqrt(ms + EPS)) * fg_ref[...]
        o_ref[...] = out

    @pl.when(i < n_lat_tiles)
    def _():
        slot = slot_ref[...]
        gate = gate_ref[...]
        iota_c = lax.broadcasted_iota(I32, (1, cap_lat), 1).astype(F32)
        acc = jnp.zeros((tt, d), F32)
        for e in range(n_exp):
            pt = jnp.where(slot[:, e:e + 1] == iota_c, gate[:, e:e + 1], 0.0).astype(BF16)
            acc = acc + jnp.dot(pt, y_ref[e, 0:cap_lat, :], preferred_element_type=F32)
        finish(acc)

    if cap_ctx:
        @pl.when(i >= n_lat_tiles)
        def _():
            slot = slot_ref[...]
            gate = gate_ref[...]
            iota_c = lax.broadcasted_iota(I32, (1, n_exp * cap_ctx), 1).astype(F32)
            pt = jnp.zeros((tt, n_exp * cap_ctx), F32)
            for e in range(n_exp):
                pt = pt + jnp.where(slot[:, e:e + 1] + float(e * cap_ctx) == iota_c,
                                    gate[:, e:e + 1], 0.0)
            yc = y_ref[:, cap_lat:cap_lat + cap_ctx, :].reshape(n_exp * cap_ctx, d)
            finish(jnp.dot(pt.astype(BF16), yc, preferred_element_type=F32))


def _combine(y, slot, gate, h, mod, final_g, *, n_lat, n_out, tt, final_norm):
    n_exp, bsz, r, d = y.shape
    t = slot.shape[1]
    cap_lat = EC_CAPACITY * n_lat // n_exp
    cap_ctx = EC_CAPACITY * (t - n_lat) // n_exp
    kern = functools.partial(_combine_kernel, n_lat=n_lat, ctx_row=bsz, cap_lat=cap_lat,
                             cap_ctx=cap_ctx, final_norm=final_norm)
    return pl.pallas_call(
        kern,
        grid=(bsz, n_out // tt),
        in_specs=[pl.BlockSpec((n_exp, None, r, d), lambda b, i: (0, b, 0, 0)),
                  pl.BlockSpec((None, tt, LANES), lambda b, i: (b, i, 0)),
                  pl.BlockSpec((None, tt, LANES), lambda b, i: (b, i, 0)),
                  pl.BlockSpec((None, tt, d), lambda b, i: (b, i, 0)),
                  pl.BlockSpec((MOD_ROWS, d), lambda b, i: (0, 5)),
                  pl.BlockSpec((1, d), lambda b, i: (0, 0))],
        out_specs=pl.BlockSpec((None, tt, d), lambda b, i: (b, i, 0)),
        out_shape=jax.ShapeDtypeStruct((bsz, n_out, d), F32),
        compiler_params=_cparams(("arbitrary", "arbitrary")),
        name="moe_combine",
    )(y, slot, gate, h, mod, final_g)


def _moe(h, g2, mod, w_router, w_gate_up, w_down, final_g, *, layer, n_lat, final_norm):
    bsz, t, d = h.shape
    n_exp = w_router.shape[-1]
    wr = jnp.pad(w_router, ((0, 0), (0, LANES - n_exp)))
    m, p = _norm_probs(h, g2, mod, wr, n_lat=n_lat, n_exp=n_exp, tt=_pick(t, (768, 512, 256)))
    slot, gate, slot_t = _select_tokens(p, n_lat=n_lat, n_exp=n_exp)
    xg = _gather(m, slot_t, n_lat=n_lat, n_exp=n_exp, eg=4)
    r = xg.shape[2]
    y = _ffn(xg.reshape(n_exp, bsz * r, d), w_gate_up, w_down, layer=layer, fc=256,
             rc=_pick(bsz * r, (768, 1024, 512, 256, 128, 64, 32, 16)))
    return _combine(y.reshape(n_exp, bsz, r, d), slot, gate, h, mod, final_g,
                    n_lat=n_lat, n_out=t, tt=256, final_norm=final_norm)


def kernel(x, c, ctx, c_ctx, ada_w, ada_b, norm1_g, norm2_g, final_g, attn_w_qkv, attn_lq1, attn_lk1, attn_lq2, attn_lk2, attn_subln_g, attn_w_o, lru_w_in, lru_b_in, lru_conv_w, lru_conv_b, lru_w_gates, lru_b_gates, lru_lambda, lru_w_out, moe_w_router, moe_w_gate_up, moe_w_down):
    bsz, n_lat, d = x.shape
    n_ctx = ctx.shape[1]
    depth = ada_w.shape[0]
    assert bsz < MOD_ROWS and d % (2 * HEAD_W) == 0 and n_lat % GRID_W == 0

    cc = jnp.concatenate([c, c_ctx[None, :], jnp.zeros((MOD_ROWS - bsz - 1, d), F32)], axis=0)
    mods = _mod_tables(cc, ada_w, ada_b)
    h = jnp.concatenate([x, ctx], axis=1)
    row = lambda v: v.reshape(1, -1)

    for i in range(depth):
        last = i == depth - 1
        mod = mods[i]
        j = i // N_MIXERS
        if i % N_MIXERS == 0:
            lam_init = 0.8 - 0.6 * math.exp(-0.3 * i)
            w = attn_w_qkv[j]
            wqk = w[:, :2 * d].astype(BF16)
            wvt = w[:, 2 * d:].T.astype(BF16)
            cos, s1, s2 = _rope_tables(n_lat, n_ctx)
            t_all = n_lat + n_ctx
            qk, vt = _qkv(h, row(norm1_g[i]), mod, wqk, wvt, cos, s1, s2, n_lat=n_lat,
                          tt=_pick(t_all, (768, 256)))
            lamv = jnp.stack([attn_lq1[j], attn_lk1[j], attn_lq2[j], attn_lk2[j]], axis=0)
            sg = attn_subln_g[j].reshape(HEAD_W, 1)
            o = _attention(qk, vt, lamv, sg, None, q_row0=0, n_q=n_lat, k_row0=0,
                           n_k=n_lat + n_ctx, tq=256, kc=256, nh=2, lam_init=lam_init)
            n_rows = n_lat
            if not last:
                o = _attention(qk, vt, lamv, sg, o, q_row0=n_lat, n_q=n_ctx, k_row0=n_lat,
                               n_k=n_ctx, tq=n_ctx, kc=n_ctx, nh=2, lam_init=lam_init)
                n_rows = n_lat + n_ctx
            h = _proj_res(o, attn_w_o[j].astype(BF16), h, mod, gate_chunk=2, n_rows=n_rows,
                          n_lat=n_lat, tt=_pick(n_rows, (768, 512, 256)))
        else:
            gy, xr = _lru_in(h, row(norm1_g[i]), mod, lru_w_in[j].astype(BF16),
                             row(lru_b_in[j]), n_lat=n_lat, tt=_pick(n_lat + n_ctx, (768, 256)))
            nblk = d // LRU_BW
            u = _lru_core(xr, gy, lru_conv_w[j], row(lru_conv_b[j]),
                          lru_w_gates[j].astype(BF16),
                          lru_b_gates[j].reshape(2, nblk, 1, 2 * LRU_BW), lru_lambda[j],
                          n_lat=n_lat)
            n_rows = n_lat if last else n_lat + n_ctx
            assert last, "context output of the recurrent mixer is only needed in non-final layers"
            h = _proj_res(u, lru_w_out[j].astype(BF16), h, mod, gate_chunk=2, n_rows=n_rows,
                          n_lat=n_lat, tt=_pick(n_rows, (1024, 512, 256)))
        h = _moe(h, row(norm2_g[i]), mod, moe_w_router[i], moe_w_gate_up, moe_w_down,
                 row(final_g), layer=i, n_lat=n_lat, final_norm=last)
    return h
```

```python
import functools
import math

import jax
import jax.numpy as jnp
from jax import lax
from jax.experimental import pallas as pl
from jax.experimental.pallas import tpu as pltpu

F32 = jnp.float32
BF16 = jnp.bfloat16
I32 = jnp.int32

EPS = 1e-6
GRID_W = 64
ROPE_THETA = 10000.0
LRU_C = 8.0
EC_CAPACITY = 2
N_MIXERS = 2
HEAD_W = 128
LRU_BW = 128
LRU_SEG = 32
LRU_PITCH = 40
CONV_W = 4
MOD_ROWS = 16
PREFIX_BLOCK = 256
LANES = 128
SUBLANES = 8
VMEM_LIMIT = 56 * 2**20


def _pick(n, cands):
    return next(c for c in cands if n % c == 0)


def _cparams(sem, flags=None):
    return pltpu.CompilerParams(dimension_semantics=sem, vmem_limit_bytes=VMEM_LIMIT, flags=flags)


def _sigmoid(x):
    return 1.0 / (1.0 + jnp.exp(-x))


def _gelu_tanh(x):
    return 0.5 * x * (1.0 + jnp.tanh(math.sqrt(2.0 / math.pi) * (x + 0.044715 * (x * x * x))))


def _row_mod(mod_ref, b, ctx_row, row0, nrows, n_lat):
    mb = mod_ref[pl.ds(b, 1), :]
    mc = mod_ref[ctx_row:ctx_row + 1, :]
    rows = row0 + lax.broadcasted_iota(I32, (nrows, 1), 0)
    return jnp.where(rows < n_lat, mb, mc)


def _ln_mod(x, g, shift, scale):
    ms = jnp.mean(x * x, axis=-1, keepdims=True)
    y = (x * lax.rsqrt(ms + EPS)) * g
    return y * (1.0 + scale) + shift


def _mod_kernel(c_ref, w_ref, b_ref, o_ref):
    c = c_ref[...]
    s = c * _sigmoid(c)
    o_ref[...] = jnp.dot(s, w_ref[...], preferred_element_type=F32,
                         precision=lax.Precision.HIGHEST) + b_ref[...]


def _mod_tables(cc, ada_w, ada_b):
    depth, d, n6 = ada_w.shape
    tn = _pick(n6, (1536, 768, 384))
    return pl.pallas_call(
        _mod_kernel,
        grid=(depth, n6 // tn),
        in_specs=[pl.BlockSpec((MOD_ROWS, d), lambda i, j: (0, 0)),
                  pl.BlockSpec((None, d, tn), lambda i, j: (i, 0, j)),
                  pl.BlockSpec((None, 1, tn), lambda i, j: (i, 0, j))],
        out_specs=pl.BlockSpec((None, MOD_ROWS, tn), lambda i, j: (i, 0, j)),
        out_shape=jax.ShapeDtypeStruct((depth, MOD_ROWS, n6), F32),
        compiler_params=_cparams(("arbitrary", "arbitrary")),
        name="mod_tables",
    )(cc, ada_w, ada_b.reshape(depth, 1, n6))


def _qkv_kernel(h_ref, g_ref, sh_ref, sc_ref, wqk_ref, wvt_ref, cos_ref, s1_ref, s2_ref,
                qk_ref, vt_ref, *, n_lat, ctx_row):
    b = pl.program_id(0)
    i = pl.program_id(1)
    tt, d = h_ref.shape
    shift = _row_mod(sh_ref, b, ctx_row, i * tt, tt, n_lat)
    scale = _row_mod(sc_ref, b, ctx_row, i * tt, tt, n_lat)
    xn = _ln_mod(h_ref[...], g_ref[...], shift, scale).astype(BF16)
    cw = 2 * HEAD_W
    cos = jnp.concatenate([cos_ref[...]] * 2, axis=1)
    s1 = jnp.concatenate([s1_ref[...]] * 2, axis=1)
    s2 = jnp.concatenate([s2_ref[...]] * 2, axis=1)
    for n in range(2 * d // cw):
        acc = jnp.dot(xn, wqk_ref[:, n * cw:(n + 1) * cw], preferred_element_type=F32)
        r = acc * cos + pltpu.roll(acc, cw - 16, 1) * s1 + pltpu.roll(acc, 16, 1) * s2
        if n * cw < d:
            r = r * ((0.5 * HEAD_W) ** -0.5 * math.log2(math.e))
        qk_ref[:, n * cw:(n + 1) * cw] = r.astype(BF16)
    for n in range(d // cw):
        vt = lax.dot_general(wvt_ref[n * cw:(n + 1) * cw, :], xn, (((1,), (1,)), ((), ())),
                             preferred_element_type=F32)
        vt_ref[n * cw:(n + 1) * cw, :] = vt.astype(BF16)


def _qkv(h, g, mod, wqk, wvt, cos, s1, s2, *, n_lat, tt):
    bsz, t, d = h.shape
    kern = functools.partial(_qkv_kernel, n_lat=n_lat, ctx_row=bsz)
    return pl.pallas_call(
        kern,
        grid=(bsz, t // tt),
        in_specs=[pl.BlockSpec((None, tt, d), lambda b, i: (b, i, 0)),
                  pl.BlockSpec((1, d), lambda b, i: (0, 0)),
                  pl.BlockSpec((MOD_ROWS, d), lambda b, i: (0, 0)),
                  pl.BlockSpec((MOD_ROWS, d), lambda b, i: (0, 1)),
                  pl.BlockSpec((d, 2 * d), lambda b, i: (0, 0)),
                  pl.BlockSpec((d, d), lambda b, i: (0, 0)),
                  pl.BlockSpec((tt, HEAD_W), lambda b, i: (i, 0)),
                  pl.BlockSpec((tt, HEAD_W), lambda b, i: (i, 0)),
                  pl.BlockSpec((tt, HEAD_W), lambda b, i: (i, 0))],
        out_specs=[pl.BlockSpec((None, tt, 2 * d), lambda b, i: (b, i, 0)),
                   pl.BlockSpec((None, None, d, tt), lambda b, i: (b, i, 0, 0))],
        out_shape=[jax.ShapeDtypeStruct((bsz, t, 2 * d), BF16),
                   jax.ShapeDtypeStruct((bsz, t // tt, d, tt), BF16)],
        compiler_params=_cparams(("arbitrary", "arbitrary")),
        name="qkv_rope",
    )(h, g, mod, mod, wqk, wvt, cos, s1, s2)


def _rope_tables(n_lat, n_ctx):
    freqs = HEAD_W // 8
    rows = n_lat // GRID_W
    row = jnp.repeat(jnp.arange(rows), GRID_W).astype(F32)
    col = jnp.tile(jnp.arange(GRID_W), rows).astype(F32)
    inv = ROPE_THETA ** (-(jnp.arange(freqs, dtype=F32) * 2.0) / (2 * freqs))
    lane = jnp.arange(HEAD_W)
    dd = lane % (HEAD_W // 2)
    axis = dd // (2 * freqs)
    half = (dd % (2 * freqs)) // freqs
    f = dd % freqs
    pos = jnp.where(axis[None, :] == 0, row[:, None], col[:, None])
    ang = pos * inv[f][None, :]
    cos = jnp.cos(ang)
    sin = jnp.sin(ang)
    s1 = jnp.where(half[None, :] == 0, -sin, 0.0)
    s2 = jnp.where(half[None, :] == 1, sin, 0.0)
    pad = lambda a, v: jnp.concatenate([a, jnp.full((n_ctx, HEAD_W), v, F32)], axis=0)
    return pad(cos, 1.0), pad(s1, 0.0), pad(s2, 0.0)


def _diff_lambda(lam_ref, lam_init):
    lv = lam_ref[...]
    return (jnp.exp(jnp.sum(lv[0:1, :] * lv[1:2, :], axis=1, keepdims=True))
            - jnp.exp(jnp.sum(lv[2:3, :] * lv[3:4, :], axis=1, keepdims=True)) + lam_init)


def _split_q(q_ref, hh):
    q = q_ref[:, hh * HEAD_W:(hh + 1) * HEAD_W].astype(F32)
    lane = lax.broadcasted_iota(I32, q.shape, 1)
    return jnp.concatenate([jnp.where(lane < HEAD_W // 2, q, 0.0),
                            jnp.where(lane >= HEAD_W // 2, q, 0.0)], axis=0).astype(BF16)


def _attn_finish(o_ref, sg_ref, hh, l8, acc, lam, lam_init):
    tq = o_ref.shape[0]
    r = 1.0 / jnp.sum(l8, axis=0, keepdims=True)
    ot = acc[:, :tq] * r[:, :tq] - lam * (acc[:, tq:] * r[:, tq:])
    msq = jnp.mean(ot * ot, axis=0, keepdims=True)
    ot = (ot * lax.rsqrt(msq + EPS)) * sg_ref[...] * (1.0 - lam_init)
    o_ref[:, hh * HEAD_W:(hh + 1) * HEAD_W] = ot.T.astype(BF16)


def _attn_pipe_kernel(lam_ref, sg_ref, q_ref, k_ref, vt_ref, o_ref, s0_scr, s1_scr, m0_scr, m1_scr,
                      acc_scr, *, lam_init):
    f = pl.program_id(1)
    tq = q_ref.shape[0]
    nh = q_ref.shape[1] // HEAD_W
    ng, _, gk = vt_ref.shape

    @pl.when(f == 0)
    def _():
        s1_scr[...] = jnp.zeros_like(s1_scr)
        m1_scr[...] = jnp.zeros_like(m1_scr)

    def step(sw_scr, mw_scr, sr_scr, mr_scr):
        lam = _diff_lambda(lam_ref, lam_init)
        qbd = [_split_q(q_ref, hh) for hh in range(nh)]
        m_prev = [mr_scr[hh] for hh in range(nh)]
        acc_scr[...] = jnp.zeros_like(acc_scr)

        def body(g, carry):
            m8s, l8s = carry
            base = pl.multiple_of(g * gk, gk)
            new_m8, new_l8 = [], []
            for hh in range(nh):
                cols = slice(hh * HEAD_W, (hh + 1) * HEAD_W)
                s = lax.dot_general(k_ref[pl.ds(base, gk), cols], qbd[hh],
                                    (((1,), (1,)), ((), ())), preferred_element_type=F32)
                sw_scr[hh, pl.ds(base, gk), :] = s
                new_m8.append(jnp.maximum(
                    m8s[hh], jnp.max(s.reshape(gk // SUBLANES, SUBLANES, 2 * tq), axis=0)))
                e = jnp.exp2(sr_scr[hh, pl.ds(base, gk), :] - m_prev[hh])
                new_l8.append(
                    l8s[hh] + jnp.sum(e.reshape(gk // SUBLANES, SUBLANES, 2 * tq), axis=0))
                acc_scr[hh] += jnp.dot(vt_ref[g, cols, :], e.astype(BF16),
                                       preferred_element_type=F32)
            return tuple(new_m8), tuple(new_l8)

        init = (tuple(jnp.full((SUBLANES, 2 * tq), -jnp.inf, F32) for _ in range(nh)),
                tuple(jnp.zeros((SUBLANES, 2 * tq), F32) for _ in range(nh)))
        m8s, l8s = init
        for g in range(ng):
            m8s, l8s = body(g, (m8s, l8s))
        for hh in range(nh):
            mw_scr[hh] = jnp.max(m8s[hh], axis=0, keepdims=True)
            _attn_finish(o_ref, sg_ref, hh, l8s[hh], acc_scr[hh], lam, lam_init)

    @pl.when(lax.rem(f, 2) == 0)
    def _():
        step(s0_scr, m0_scr, s1_scr, m1_scr)

    @pl.when(lax.rem(f, 2) == 1)
    def _():
        step(s1_scr, m1_scr, s0_scr, m0_scr)


def _attention_lat(qk, vt, lamv, sg, *, n_lat, tq, nh, lam_init):
    bsz, t, d2 = qk.shape
    d = d2 // 2
    hw = nh * HEAD_W
    heads = d // hw
    ng, gk = vt.shape[1], vt.shape[3]
    n_tiles = n_lat // tq
    last = heads * n_tiles - 1
    cur = lambda f: jnp.minimum(f, last)
    prev = lambda f: jnp.maximum(f - 1, 0)
    kern = functools.partial(_attn_pipe_kernel, lam_init=lam_init)
    return pl.pallas_call(
        kern,
        grid=(bsz, heads * n_tiles + 1),
        in_specs=[pl.BlockSpec((4, HEAD_W // 2), lambda b, f: (0, 0)),
                  pl.BlockSpec((HEAD_W, 1), lambda b, f: (0, 0)),
                  pl.BlockSpec((None, tq, hw), lambda b, f: (b, cur(f) % n_tiles, cur(f) // n_tiles)),
                  pl.BlockSpec((None, t, hw), lambda b, f: (b, 0, heads + cur(f) // n_tiles)),
                  pl.BlockSpec((None, ng, hw, gk), lambda b, f: (b, 0, prev(f) // n_tiles, 0))],
        out_specs=pl.BlockSpec((None, tq, hw),
                               lambda b, f: (b, prev(f) % n_tiles, prev(f) // n_tiles)),
        out_shape=jax.ShapeDtypeStruct((bsz, n_lat, d), BF16),
        scratch_shapes=[pltpu.VMEM((nh, t, 2 * tq), F32), pltpu.VMEM((nh, t, 2 * tq), F32),
                        pltpu.VMEM((nh, 1, 2 * tq), F32), pltpu.VMEM((nh, 1, 2 * tq), F32),
                        pltpu.VMEM((nh, HEAD_W, 2 * tq), F32)],
        compiler_params=_cparams(("arbitrary", "arbitrary")),
        name="diff_attn_lat",
    )(lamv, sg, qk, qk, vt)


def _attn_kernel(lam_ref, sg_ref, q_ref, k_ref, vt_ref, o_ref, s_scr, *, kc, v_off, lam_init):
    tq = q_ref.shape[0]
    tk = k_ref.shape[0]
    nh = q_ref.shape[1] // HEAD_W
    lam = _diff_lambda(lam_ref, lam_init)

    chunks = range(tk // kc)
    ms = []
    for hh in range(nh):
        qb = _split_q(q_ref, hh)
        m8 = jnp.full((SUBLANES, 2 * tq), -jnp.inf, F32)
        for c in chunks:
            s = lax.dot_general(k_ref[c * kc:(c + 1) * kc, hh * HEAD_W:(hh + 1) * HEAD_W], qb,
                                (((1,), (1,)), ((), ())), preferred_element_type=F32)
            s_scr[hh, c * kc:(c + 1) * kc, :] = s
            m8 = jnp.maximum(m8, jnp.max(s.reshape(kc // SUBLANES, SUBLANES, 2 * tq), axis=0))
        ms.append(jnp.max(m8, axis=0, keepdims=True))
    for hh in range(nh):
        l8 = jnp.zeros((SUBLANES, 2 * tq), F32)
        acc = jnp.zeros((HEAD_W, 2 * tq), F32)
        for c in chunks:
            e = jnp.exp2(s_scr[hh, c * kc:(c + 1) * kc, :] - ms[hh])
            l8 = l8 + jnp.sum(e.reshape(kc // SUBLANES, SUBLANES, 2 * tq), axis=0)
            acc = acc + jnp.dot(
                vt_ref[hh * HEAD_W:(hh + 1) * HEAD_W, v_off + c * kc:v_off + (c + 1) * kc],
                e.astype(BF16), preferred_element_type=F32)
        _attn_finish(o_ref, sg_ref, hh, l8, acc, lam, lam_init)


def _attention_ctx(qk, vt, lamv, sg, *, n_lat, nh, lam_init):
    bsz, t, d2 = qk.shape
    d = d2 // 2
    hw = nh * HEAD_W
    heads = d // hw
    n_ctx = t - n_lat
    gk = vt.shape[3]
    assert n_lat % n_ctx == 0 and n_lat % gk + n_ctx <= gk
    rb = n_lat // n_ctx
    kern = functools.partial(_attn_kernel, kc=n_ctx, v_off=n_lat % gk, lam_init=lam_init)
    return pl.pallas_call(
        kern,
        grid=(bsz, heads),
        in_specs=[pl.BlockSpec((4, HEAD_W // 2), lambda b, h: (0, 0)),
                  pl.BlockSpec((HEAD_W, 1), lambda b, h: (0, 0)),
                  pl.BlockSpec((None, n_ctx, hw), lambda b, h: (b, rb, h)),
                  pl.BlockSpec((None, n_ctx, hw), lambda b, h: (b, rb, heads + h)),
                  pl.BlockSpec((None, None, hw, gk), lambda b, h: (b, n_lat // gk, h, 0))],
        out_specs=pl.BlockSpec((None, n_ctx, hw), lambda b, h: (b, 0, h)),
        out_shape=jax.ShapeDtypeStruct((bsz, n_ctx, d), BF16),
        scratch_shapes=[pltpu.VMEM((nh, n_ctx, 2 * n_ctx), F32)],
        compiler_params=_cparams(("arbitrary", "arbitrary")),
        name="diff_attn_ctx",
    )(lamv, sg, qk, qk, vt)


def _proj_res_kernel(u_ref, w_ref, h_ref, gate_ref, o_ref, *, n_lat, ctx_row, nc):
    b = pl.program_id(0)
    i = pl.program_id(1)
    tt, d = h_ref.shape
    gate = _row_mod(gate_ref, b, ctx_row, i * tt, tt, n_lat)
    u = u_ref[...]
    for n in range(d // nc):
        y = jnp.dot(u, w_ref[:, n * nc:(n + 1) * nc], preferred_element_type=F32)
        o_ref[:, n * nc:(n + 1) * nc] = (h_ref[:, n * nc:(n + 1) * nc]
                                         + gate[:, n * nc:(n + 1) * nc] * y)


def _proj_res(u, w, h, mod, *, gate_chunk, n_rows, n_lat, tt):
    bsz, _, k = u.shape
    d = h.shape[-1]
    kern = functools.partial(_proj_res_kernel, n_lat=n_lat, ctx_row=bsz, nc=256)
    return pl.pallas_call(
        kern,
        grid=(bsz, n_rows // tt),
        in_specs=[pl.BlockSpec((None, tt, k), lambda b, i: (b, i, 0)),
                  pl.BlockSpec((k, d), lambda b, i: (0, 0)),
                  pl.BlockSpec((None, tt, d), lambda b, i: (b, i, 0)),
                  pl.BlockSpec((MOD_ROWS, d), lambda b, i: (0, gate_chunk))],
        out_specs=pl.BlockSpec((None, tt, d), lambda b, i: (b, i, 0)),
        out_shape=jax.ShapeDtypeStruct((bsz, n_rows, d), F32),
        compiler_params=_cparams(("arbitrary", "arbitrary")),
        name="proj_residual",
    )(u, w, h, mod)


def _lru_in_kernel(h_ref, g_ref, sh_ref, sc_ref, w_ref, bias_ref, gy_ref, xr_ref, *,
                   n_lat, ctx_row, nc):
    b = pl.program_id(0)
    i = pl.program_id(1)
    tt, d = h_ref.shape
    shift = _row_mod(sh_ref, b, ctx_row, i * tt, tt, n_lat)
    scale = _row_mod(sc_ref, b, ctx_row, i * tt, tt, n_lat)
    xn = _ln_mod(h_ref[...], g_ref[...], shift, scale).astype(BF16)
    for n in range(d // nc):
        y = jnp.dot(xn, w_ref[:, n * nc:(n + 1) * nc], preferred_element_type=F32)
        y = y + bias_ref[:, n * nc:(n + 1) * nc]
        gy_ref[:, n * nc:(n + 1) * nc] = _gelu_tanh(y).astype(BF16)
    for n in range(d // nc):
        x = jnp.dot(xn, w_ref[:, d + n * nc:d + (n + 1) * nc], preferred_element_type=F32)
        xr_ref[:, n * nc:(n + 1) * nc] = x + bias_ref[:, d + n * nc:d + (n + 1) * nc]


def _lru_in(h, g, mod, w_in, b_in, *, n_lat, tt):
    bsz, t, d = h.shape
    kern = functools.partial(_lru_in_kernel, n_lat=n_lat, ctx_row=bsz, nc=256)
    return pl.pallas_call(
        kern,
        grid=(bsz, t // tt),
        in_specs=[pl.BlockSpec((None, tt, d), lambda b, i: (b, i, 0)),
                  pl.BlockSpec((1, d), lambda b, i: (0, 0)),
                  pl.BlockSpec((MOD_ROWS, d), lambda b, i: (0, 0)),
                  pl.BlockSpec((MOD_ROWS, d), lambda b, i: (0, 1)),
                  pl.BlockSpec((d, 2 * d), lambda b, i: (0, 0)),
                  pl.BlockSpec((1, 2 * d), lambda b, i: (0, 0))],
        out_specs=[pl.BlockSpec((None, tt, d), lambda b, i: (b, i, 0)),
                   pl.BlockSpec((None, tt, d), lambda b, i: (b, i, 0))],
        out_shape=[jax.ShapeDtypeStruct((bsz, t, d), BF16),
                   jax.ShapeDtypeStruct((bsz, t, d), F32)],
        compiler_params=_cparams(("arbitrary", "arbitrary")),
        name="lru_in_proj",
    )(h, g, mod, mod, w_in, b_in)


def _lru_kernel(xr_ref, gy_ref, cw_ref, cb_ref, wg_ref, bg_ref, lam_ref, u_ref,
                xp_scr, hf_scr, hb_scr, *, n_lat):
    t, cw = xr_ref.shape
    n_ctx = t - n_lat
    seg, pitch, nsub = LRU_SEG, LRU_PITCH, SUBLANES
    rc = seg * nsub
    nseg_lat, nseg_ctx = n_lat // seg, n_ctx // seg
    lat0, ctx0 = 1, nseg_lat + 2
    zseg = jnp.zeros((seg, cw), F32)
    for s in (0, nseg_lat + 1, nseg_lat + nseg_ctx + 2):
        xp_scr[s * pitch:s * pitch + seg, :] = zseg
    for s in range(nseg_lat):
        xp_scr[(lat0 + s) * pitch:(lat0 + s) * pitch + seg, :] = xr_ref[s * seg:(s + 1) * seg, :]
    for s in range(nseg_ctx):
        xp_scr[(ctx0 + s) * pitch:(ctx0 + s) * pitch + seg, :] = (
            xr_ref[n_lat + s * seg:n_lat + (s + 1) * seg, :])
    lam = lam_ref[...]
    sp = jnp.maximum(-lam, 0.0) + jnp.log(1.0 + jnp.exp(-jnp.abs(lam)))
    w = cw_ref[...]
    cb = cb_ref[...]

    def rows8(ref, start):
        return ref[pl.ds(start, nsub, stride=pitch), :]

    def chunk(pb, carry, d, reverse, out_scr):
        x = [rows8(xp_scr, pb + g) for g in range(seg)]
        xm1 = [rows8(xp_scr, pb - (pitch - seg) - 1)] + x[:-1]
        xm2 = [rows8(xp_scr, pb - (pitch - seg) - 2), xm1[0]] + x[:-2]
        xp1 = x[1:] + [rows8(xp_scr, pb + pitch)]
        xc = jnp.concatenate(
            [cb + w[0:1, :] * xm2[g] + w[1:2, :] * xm1[g] + w[2:3, :] * x[g] + w[3:4, :] * xp1[g]
             for g in range(seg)], axis=0)
        gpre = jnp.dot(xc.astype(BF16), wg_ref[d, 0], preferred_element_type=F32) + bg_ref[d, 0]
        r = _sigmoid(gpre[:, :LRU_BW])
        ig = _sigmoid(gpre[:, LRU_BW:])
        a = jnp.exp((-LRU_C * sp[d:d + 1, :]) * r)
        bt = jnp.sqrt(1.0 - a * a) * (ig * xc)
        a3 = a.reshape(seg, nsub, cw)
        b3 = bt.reshape(seg, nsub, cw)
        h = jnp.zeros((nsub, cw), F32)
        p = jnp.ones((nsub, cw), F32)
        hl, pl_ = [None] * seg, [None] * seg
        for g in (range(seg - 1, -1, -1) if reverse else range(seg)):
            h = a3[g] * h + b3[g]
            p = a3[g] * p
            hl[g], pl_[g] = h, p
        entry = [None] * nsub
        for j in (range(nsub - 1, -1, -1) if reverse else range(nsub)):
            entry[j] = carry
            carry = h[j:j + 1, :] + p[j:j + 1, :] * carry
        h_in = jnp.concatenate(entry, axis=0)
        for g in range(seg):
            out_scr[pl.ds(pb + g, nsub, stride=pitch), :] = hl[g] + pl_[g] * h_in
        return carry

    lat_pb = lambda c: lat0 * pitch + c * (nsub * pitch)
    ctx_pb = lambda c: (ctx0 + c * nsub) * pitch
    n_lat_chunks, n_ctx_chunks = n_lat // rc, n_ctx // rc
    cf = cb_ = jnp.zeros((1, cw), F32)
    for c in range(n_ctx_chunks):
        cf = chunk(ctx_pb(c), cf, 0, False, hf_scr)
        cb_ = chunk(ctx_pb(n_ctx_chunks - 1 - c), cb_, 1, True, hb_scr)

    def both(c, carries):
        return (chunk(lat_pb(c), carries[0], 0, False, hf_scr),
                chunk(lat_pb(n_lat_chunks - 1 - c), carries[1], 1, True, hb_scr))

    lax.fori_loop(0, n_lat_chunks, both, (cf, cb_))
    for s in range(nseg_lat):
        rows = slice((lat0 + s) * pitch, (lat0 + s) * pitch + seg)
        u_ref[s * seg:(s + 1) * seg, :] = (gy_ref[s * seg:(s + 1) * seg, :].astype(F32)
                                           * (hf_scr[rows, :] + hb_scr[rows, :])).astype(BF16)


def _lru_core(xr, gy, conv_w, conv_b, w_gates, b_gates, lam, *, n_lat):
    bsz, t, d = xr.shape
    cw = LRU_BW
    rows = ((t // LRU_SEG) + 3) * LRU_PITCH
    kern = functools.partial(_lru_kernel, n_lat=n_lat)
    return pl.pallas_call(
        kern,
        grid=(bsz, d // cw),
        in_specs=[pl.BlockSpec((None, t, cw), lambda b, k: (b, 0, k)),
                  pl.BlockSpec((None, n_lat, cw), lambda b, k: (b, 0, k)),
                  pl.BlockSpec((CONV_W, cw), lambda b, k: (0, k)),
                  pl.BlockSpec((1, cw), lambda b, k: (0, k)),
                  pl.BlockSpec((2, 1, LRU_BW, 2 * LRU_BW), lambda b, k: (0, k, 0, 0)),
                  pl.BlockSpec((2, 1, 1, 2 * LRU_BW), lambda b, k: (0, k, 0, 0)),
                  pl.BlockSpec((2, cw), lambda b, k: (0, k))],
        out_specs=pl.BlockSpec((None, n_lat, cw), lambda b, k: (b, 0, k)),
        out_shape=jax.ShapeDtypeStruct((bsz, n_lat, d), BF16),
        scratch_shapes=[pltpu.VMEM((rows, cw), F32)] * 3,
        compiler_params=_cparams(("arbitrary", "arbitrary")),
        name="lru_core",
    )(xr, gy, conv_w, conv_b, w_gates, b_gates, lam)


def _norm_probs_kernel(h_ref, g_ref, sh_ref, sc_ref, wr_ref, m_ref, p_ref, *,
                       n_lat, ctx_row, n_exp):
    b = pl.program_id(0)
    i = pl.program_id(1)
    tt, d = h_ref.shape
    shift = _row_mod(sh_ref, b, ctx_row, i * tt, tt, n_lat)
    scale = _row_mod(sc_ref, b, ctx_row, i * tt, tt, n_lat)
    m = _ln_mod(h_ref[...], g_ref[...], shift, scale)
    m_hi = m.astype(BF16)
    m_ref[...] = m_hi
    m_lo = (m - m_hi.astype(F32)).astype(BF16)
    wr = wr_ref[...]
    w_hi = wr.astype(BF16)
    w_lo = (wr - w_hi.astype(F32)).astype(BF16)
    logits = (jnp.dot(m_hi, w_hi, preferred_element_type=F32)
              + jnp.dot(m_lo, w_hi, preferred_element_type=F32)
              + jnp.dot(m_hi, w_lo, preferred_element_type=F32))
    lane = lax.broadcasted_iota(I32, logits.shape, 1)
    lg = jnp.where(lane < n_exp, logits, -jnp.inf)
    ex = jnp.exp(lg - jnp.max(lg, axis=1, keepdims=True))
    p_ref[...] = ex / jnp.sum(ex, axis=1, keepdims=True)


def _norm_probs(h, g, mod, wr, *, n_lat, n_exp, tt):
    bsz, t, d = h.shape
    kern = functools.partial(_norm_probs_kernel, n_lat=n_lat, ctx_row=bsz, n_exp=n_exp)
    return pl.pallas_call(
        kern,
        grid=(bsz, t // tt),
        in_specs=[pl.BlockSpec((None, tt, d), lambda b, i: (b, i, 0)),
                  pl.BlockSpec((1, d), lambda b, i: (0, 0)),
                  pl.BlockSpec((MOD_ROWS, d), lambda b, i: (0, 3)),
                  pl.BlockSpec((MOD_ROWS, d), lambda b, i: (0, 4)),
                  pl.BlockSpec((d, LANES), lambda b, i: (0, 0))],
        out_specs=[pl.BlockSpec((None, tt, d), lambda b, i: (b, i, 0)),
                   pl.BlockSpec((None, tt, LANES), lambda b, i: (b, i, 0))],
        out_shape=[jax.ShapeDtypeStruct((bsz, t, d), BF16),
                   jax.ShapeDtypeStruct((bsz, t, LANES), F32)],
        compiler_params=_cparams(("arbitrary", "arbitrary")),
        name="moe_norm_probs",
    )(h, g, mod, mod, wr)


def _select(p, cap, n_exp):
    n_tok = p.shape[0]
    n_rows = -(-n_exp // SUBLANES) * SUBLANES
    pt = p.T[0:n_rows, :]

    def body(i, thr):
        cand = thr | jnp.left_shift(jnp.int32(1), 29 - i)
        hit = jnp.where(pt >= lax.bitcast_convert_type(cand, F32), 1.0, 0.0)
        return jnp.where(jnp.sum(hit, axis=1, keepdims=True) >= cap, cand, thr)

    thr_col = lax.fori_loop(0, 30, body, jnp.zeros((n_rows, 1), I32))
    diag = (lax.broadcasted_iota(I32, (n_rows, LANES), 0)
            == lax.broadcasted_iota(I32, (n_rows, LANES), 1))
    thr = jnp.sum(jnp.where(diag, jnp.broadcast_to(thr_col, (n_rows, LANES)), 0),
                  axis=0, keepdims=True)
    gt = p >= pltpu.bitcast(thr + 1, F32)
    eq = (p >= pltpu.bitcast(thr, F32)) & jnp.logical_not(gt)
    blk = PREFIX_BLOCK
    ltri = jnp.where(lax.broadcasted_iota(I32, (blk, blk), 1)
                     < lax.broadcasted_iota(I32, (blk, blk), 0), 1.0, 0.0).astype(BF16)
    masks = jnp.concatenate([jnp.where(gt, 1.0, 0.0), jnp.where(eq, 1.0, 0.0)], axis=1)
    off = jnp.zeros((1, 2 * LANES), F32)
    pres = []
    for i in range(n_tok // blk):
        mb = masks[i * blk:(i + 1) * blk, :]
        pre = jnp.dot(ltri, mb.astype(BF16), preferred_element_type=F32) + off
        pres.append(pre)
        off = pre[blk - 1:blk, :] + mb[blk - 1:blk, :]
    pre = jnp.concatenate(pres, axis=0)
    pre_gt, pre_eq = pre[:, :LANES], pre[:, LANES:]
    need = cap - off[:, :LANES]
    sel = gt | (eq & (pre_eq < need))
    slot = pre_gt + jnp.minimum(pre_eq, need)
    return jnp.where(sel, slot, -1.0), jnp.where(sel, p, 0.0)


def _select_kernel(p_ref, slot_ref, gate_ref, slot_t_ref, *, n_lat, n_exp):
    t = p_ref.shape[0]
    n_ctx = t - n_lat
    slot, gate = _select(p_ref[0:n_lat, :], EC_CAPACITY * n_lat // n_exp, n_exp)
    slot_ref[0:n_lat, :] = slot
    gate_ref[0:n_lat, :] = gate
    slot_t_ref[:, 0:n_lat] = slot.T
    if n_ctx:
        slot, gate = _select(p_ref[n_lat:t, :], EC_CAPACITY * n_ctx // n_exp, n_exp)
        slot_ref[n_lat:t, :] = slot
        gate_ref[n_lat:t, :] = gate
        slot_t_ref[:, n_lat:t] = slot.T


def _select_tokens(p, *, n_lat, n_exp):
    bsz, t, _ = p.shape
    kern = functools.partial(_select_kernel, n_lat=n_lat, n_exp=n_exp)
    return pl.pallas_call(
        kern,
        grid=(bsz,),
        in_specs=[pl.BlockSpec((None, t, LANES), lambda b: (b, 0, 0))],
        out_specs=[pl.BlockSpec((None, t, LANES), lambda b: (b, 0, 0)),
                   pl.BlockSpec((None, t, LANES), lambda b: (b, 0, 0)),
                   pl.BlockSpec((None, LANES, t), lambda b: (b, 0, 0))],
        out_shape=[jax.ShapeDtypeStruct((bsz, t, LANES), F32),
                   jax.ShapeDtypeStruct((bsz, t, LANES), F32),
                   jax.ShapeDtypeStruct((bsz, LANES, t), F32)],
        compiler_params=_cparams(("arbitrary",)),
        name="moe_select",
    )(p)


def _gather_kernel(m_ref, slot_t_ref, xg_ref, *, n_lat, cap_lat, cap_ctx):
    j = pl.program_id(1)
    eg = xg_ref.shape[0]
    t = m_ref.shape[0]
    for ee in range(eg):
        row = slot_t_ref[pl.ds(j * eg + ee, 1), :]
        sl = lax.broadcasted_iota(I32, (cap_lat, 1), 0).astype(F32)
        p = jnp.where(row[:, 0:n_lat] == sl, 1.0, 0.0).astype(BF16)
        xg_ref[ee, 0:cap_lat, :] = jnp.dot(p, m_ref[0:n_lat, :],
                                           preferred_element_type=F32).astype(BF16)
        if cap_ctx:
            sc = lax.broadcasted_iota(I32, (cap_ctx, 1), 0).astype(F32)
            pc = jnp.where(row[:, n_lat:t] == sc, 1.0, 0.0).astype(BF16)
            xg_ref[ee, cap_lat:cap_lat + cap_ctx, :] = jnp.dot(
                pc, m_ref[n_lat:t, :], preferred_element_type=F32).astype(BF16)


def _gather(m, slot_t, *, n_lat, n_exp, eg):
    bsz, t, d = m.shape
    cap_lat = EC_CAPACITY * n_lat // n_exp
    cap_ctx = EC_CAPACITY * (t - n_lat) // n_exp
    r = cap_lat + cap_ctx
    kern = functools.partial(_gather_kernel, n_lat=n_lat, cap_lat=cap_lat, cap_ctx=cap_ctx)
    return pl.pallas_call(
        kern,
        grid=(bsz, n_exp // eg),
        in_specs=[pl.BlockSpec((None, t, d), lambda b, j: (b, 0, 0)),
                  pl.BlockSpec((None, LANES, t), lambda b, j: (b, 0, 0))],
        out_specs=pl.BlockSpec((eg, None, r, d), lambda b, j: (j, b, 0, 0)),
        out_shape=jax.ShapeDtypeStruct((n_exp, bsz, r, d), BF16),
        compiler_params=_cparams(("arbitrary", "arbitrary")),
        name="moe_gather",
    )(m, slot_t)


def _ffn_kernel(x_ref, wg_ref, wu_ref, wd_ref, y_ref, acc_scr, *, rc):
    j = pl.program_id(1)
    nj = pl.num_programs(1)
    rows = x_ref.shape[0]

    @pl.when(j == 0)
    def _():
        acc_scr[...] = jnp.zeros_like(acc_scr)

    wg = wg_ref[...].astype(BF16)
    wu = wu_ref[...].astype(BF16)
    wd = wd_ref[...].astype(BF16)

    for c in range(rows // rc):
        x = x_ref[c * rc:(c + 1) * rc, :]
        g = jnp.dot(x, wg, preferred_element_type=F32)
        u = jnp.dot(x, wu, preferred_element_type=F32)
        hid = ((g * _sigmoid(g)) * u).astype(BF16)
        acc_scr[c * rc:(c + 1) * rc, :] += jnp.dot(hid, wd, preferred_element_type=F32)

    @pl.when(j == nj - 1)
    def _():
        y_ref[...] = acc_scr[...].astype(BF16)


def _ffn(xg, w_gate_up, w_down, *, layer, fc, rc):
    n_exp, rows, d = xg.shape
    f = w_down.shape[2]
    nj = f // fc
    kern = functools.partial(_ffn_kernel, rc=rc)
    return pl.pallas_call(
        kern,
        grid=(n_exp, nj),
        in_specs=[pl.BlockSpec((None, rows, d), lambda e, j: (e, 0, 0)),
                  pl.BlockSpec((None, None, d, fc), lambda e, j: (layer, e, 0, j)),
                  pl.BlockSpec((None, None, d, fc), lambda e, j: (layer, e, 0, nj + j)),
                  pl.BlockSpec((None, None, fc, d), lambda e, j: (layer, e, j, 0))],
        out_specs=pl.BlockSpec((None, rows, d), lambda e, j: (e, 0, 0)),
        out_shape=jax.ShapeDtypeStruct((n_exp, rows, d), BF16),
        scratch_shapes=[pltpu.VMEM((rows, d), F32)],
        compiler_params=_cparams(("arbitrary", "arbitrary")),
        name="moe_ffn",
    )(xg, w_gate_up, w_gate_up, w_down)


def _combine_kernel(y_ref, slot_ref, gate_ref, h_ref, g2_ref, fg_ref, o_ref, *,
                    n_lat, ctx_row, cap_lat, cap_ctx, final_norm):
    b = pl.program_id(0)
    i = pl.program_id(1)
    tt, d = h_ref.shape
    n_exp = y_ref.shape[0]
    n_lat_tiles = n_lat // tt

    def finish(acc):
        gate2 = _row_mod(g2_ref, b, ctx_row, i * tt, tt, n_lat)
        out = h_ref[...] + gate2 * acc
        if final_norm:
            ms = jnp.mean(out * out, axis=-1, keepdims=True)
            out = (out * lax.rsqrt(ms + EPS)) * fg_ref[...]
        o_ref[...] = out

    @pl.when(i < n_lat_tiles)
    def _():
        slot = slot_ref[...]
        gate = gate_ref[...]
        iota_c = lax.broadcasted_iota(I32, (1, cap_lat), 1).astype(F32)
        acc = jnp.zeros((tt, d), F32)
        for e in range(n_exp):
            pt = jnp.where(slot[:, e:e + 1] == iota_c, gate[:, e:e + 1], 0.0).astype(BF16)
            acc = acc + jnp.dot(pt, y_ref[e, 0:cap_lat, :], preferred_element_type=F32)
        finish(acc)

    if cap_ctx:
        @pl.when(i >= n_lat_tiles)
        def _():
            slot = slot_ref[...]
            gate = gate_ref[...]
            iota_c = lax.broadcasted_iota(I32, (1, n_exp * cap_ctx), 1).astype(F32)
            pt = jnp.zeros((tt, n_exp * cap_ctx), F32)
            for e in range(n_exp):
                pt = pt + jnp.where(slot[:, e:e + 1] + float(e * cap_ctx) == iota_c,
                                    gate[:, e:e + 1], 0.0)
            yc = y_ref[:, cap_lat:cap_lat + cap_ctx, :].reshape(n_exp * cap_ctx, d)
            finish(jnp.dot(pt.astype(BF16), yc, preferred_element_type=F32))


def _combine(y, slot, gate, h, mod, final_g, *, n_lat, n_out, tt, final_norm):
    n_exp, bsz, r, d = y.shape
    t = slot.shape[1]
    cap_lat = EC_CAPACITY * n_lat // n_exp
    cap_ctx = EC_CAPACITY * (t - n_lat) // n_exp
    kern = functools.partial(_combine_kernel, n_lat=n_lat, ctx_row=bsz, cap_lat=cap_lat,
                             cap_ctx=cap_ctx, final_norm=final_norm)
    return pl.pallas_call(
        kern,
        grid=(bsz, n_out // tt),
        in_specs=[pl.BlockSpec((n_exp, None, r, d), lambda b, i: (0, b, 0, 0)),
                  pl.BlockSpec((None, tt, LANES), lambda b, i: (b, i, 0)),
                  pl.BlockSpec((None, tt, LANES), lambda b, i: (b, i, 0)),
                  pl.BlockSpec((None, tt, d), lambda b, i: (b, i, 0)),
                  pl.BlockSpec((MOD_ROWS, d), lambda b, i: (0, 5)),
                  pl.BlockSpec((1, d), lambda b, i: (0, 0))],
        out_specs=pl.BlockSpec((None, tt, d), lambda b, i: (b, i, 0)),
        out_shape=jax.ShapeDtypeStruct((bsz, n_out, d), F32),
        compiler_params=_cparams(("arbitrary", "arbitrary")),
        name="moe_combine",
    )(y, slot, gate, h, mod, final_g)


def _moe(h, g2, mod, w_router, w_gate_up, w_down, final_g, *, layer, n_lat, final_norm):
    bsz, t, d = h.shape
    n_exp = w_router.shape[-1]
    wr = jnp.pad(w_router, ((0, 0), (0, LANES - n_exp)))
    m, p = _norm_probs(h, g2, mod, wr, n_lat=n_lat, n_exp=n_exp, tt=_pick(t, (768, 512, 256)))
    slot, gate, slot_t = _select_tokens(p, n_lat=n_lat, n_exp=n_exp)
    xg = _gather(m, slot_t, n_lat=n_lat, n_exp=n_exp, eg=4)
    r = xg.shape[2]
    y = _ffn(xg.reshape(n_exp, bsz * r, d), w_gate_up, w_down, layer=layer, fc=256,
             rc=_pick(bsz * r, (768, 1024, 512, 256, 128, 64, 32, 16)))
    return _combine(y.reshape(n_exp, bsz, r, d), slot, gate, h, mod, final_g,
                    n_lat=n_lat, n_out=t, tt=256, final_norm=final_norm)


def kernel(x, c, ctx, c_ctx, ada_w, ada_b, norm1_g, norm2_g, final_g, attn_w_qkv, attn_lq1, attn_lk1, attn_lq2, attn_lk2, attn_subln_g, attn_w_o, lru_w_in, lru_b_in, lru_conv_w, lru_conv_b, lru_w_gates, lru_b_gates, lru_lambda, lru_w_out, moe_w_router, moe_w_gate_up, moe_w_down):
    bsz, n_lat, d = x.shape
    n_ctx = ctx.shape[1]
    depth = ada_w.shape[0]
    assert bsz < MOD_ROWS and d % (2 * HEAD_W) == 0 and n_lat % GRID_W == 0

    cc = jnp.concatenate([c, c_ctx[None, :], jnp.zeros((MOD_ROWS - bsz - 1, d), F32)], axis=0)
    mods = _mod_tables(cc, ada_w, ada_b)
    h = jnp.concatenate([x, ctx], axis=1)
    row = lambda v: v.reshape(1, -1)

    for i in range(depth):
        last = i == depth - 1
        mod = mods[i]
        j = i // N_MIXERS
        if i % N_MIXERS == 0:
            lam_init = 0.8 - 0.6 * math.exp(-0.3 * i)
            w = attn_w_qkv[j]
            wqk = w[:, :2 * d].astype(BF16)
            wvt = w[:, 2 * d:].T.astype(BF16)
            cos, s1, s2 = _rope_tables(n_lat, n_ctx)
            t_all = n_lat + n_ctx
            qk, vt = _qkv(h, row(norm1_g[i]), mod, wqk, wvt, cos, s1, s2, n_lat=n_lat,
                          tt=_pick(t_all, (768, 256)))
            lamv = jnp.stack([attn_lq1[j], attn_lk1[j], attn_lq2[j], attn_lk2[j]], axis=0)
            sg = attn_subln_g[j].reshape(HEAD_W, 1)
            n_rows = n_lat if last else n_lat + n_ctx
            o = _attention_lat(qk, vt, lamv, sg, n_lat=n_lat, tq=256, nh=2, lam_init=lam_init)
            if not last:
                o_ctx = _attention_ctx(qk, vt, lamv, sg, n_lat=n_lat, nh=2, lam_init=lam_init)
                o = jnp.concatenate([o, o_ctx], axis=1)
            h = _proj_res(o, attn_w_o[j].astype(BF16), h, mod, gate_chunk=2, n_rows=n_rows,
                          n_lat=n_lat, tt=_pick(n_rows, (768, 512, 256)))
        else:
            gy, xr = _lru_in(h, row(norm1_g[i]), mod, lru_w_in[j].astype(BF16),
                             row(lru_b_in[j]), n_lat=n_lat, tt=_pick(n_lat + n_ctx, (768, 256)))
            nblk = d // LRU_BW
            u = _lru_core(xr, gy, lru_conv_w[j], row(lru_conv_b[j]),
                          lru_w_gates[j].astype(BF16),
                          lru_b_gates[j].reshape(2, nblk, 1, 2 * LRU_BW), lru_lambda[j],
                          n_lat=n_lat)
            n_rows = n_lat if last else n_lat + n_ctx
            assert last, "context output of the recurrent mixer is only needed in non-final layers"
            h = _proj_res(u, lru_w_out[j].astype(BF16), h, mod, gate_chunk=2, n_rows=n_rows,
                          n_lat=n_lat, tt=_pick(n_rows, (1024, 512, 256)))
        h = _moe(h, row(norm2_g[i]), mod, moe_w_router[i], moe_w_gate_up, moe_w_down,
                 row(final_g), layer=i, n_lat=n_lat, final_norm=last)
    return h
```

```python
import functools
import math

import jax
import jax.numpy as jnp
from jax import lax
from jax.experimental import pallas as pl
from jax.experimental.pallas import tpu as pltpu

F32 = jnp.float32
BF16 = jnp.bfloat16
I32 = jnp.int32

EPS = 1e-6
GRID_W = 64
ROPE_THETA = 10000.0
LRU_C = 8.0
EC_CAPACITY = 2
N_MIXERS = 2
HEAD_W = 128
LRU_BW = 128
LRU_SEG = 32
LRU_PITCH = 40
CONV_W = 4
MOD_ROWS = 16
PREFIX_BLOCK = 256
LANES = 128
SUBLANES = 8
VMEM_LIMIT = 56 * 2**20


def _pick(n, cands):
    return next(c for c in cands if n % c == 0)


def _cparams(sem, flags=None):
    return pltpu.CompilerParams(dimension_semantics=sem, vmem_limit_bytes=VMEM_LIMIT, flags=flags)


def _sigmoid(x):
    return 0.5 * jnp.tanh(0.5 * x) + 0.5


def _gelu_tanh(x):
    return 0.5 * x * (1.0 + jnp.tanh(math.sqrt(2.0 / math.pi) * (x + 0.044715 * (x * x * x))))


def _row_mod(mod_ref, b, ctx_row, row0, nrows, n_lat):
    mb = mod_ref[pl.ds(b, 1), :]
    mc = mod_ref[ctx_row:ctx_row + 1, :]
    rows = row0 + lax.broadcasted_iota(I32, (nrows, 1), 0)
    return jnp.where(rows < n_lat, mb, mc)


def _ln_mod(x, g, shift, scale):
    ms = jnp.mean(x * x, axis=-1, keepdims=True)
    y = (x * lax.rsqrt(ms + EPS)) * g
    return y * (1.0 + scale) + shift


def _mod_kernel(c_ref, w_ref, b_ref, o_ref):
    c = c_ref[...]
    s = c * _sigmoid(c)
    o_ref[...] = jnp.dot(s, w_ref[...], preferred_element_type=F32,
                         precision=lax.Precision.HIGHEST) + b_ref[...]


def _mod_tables(cc, ada_w, ada_b):
    depth, d, n6 = ada_w.shape
    tn = _pick(n6, (1536, 768, 384))
    return pl.pallas_call(
        _mod_kernel,
        grid=(depth, n6 // tn),
        in_specs=[pl.BlockSpec((MOD_ROWS, d), lambda i, j: (0, 0)),
                  pl.BlockSpec((None, d, tn), lambda i, j: (i, 0, j)),
                  pl.BlockSpec((None, 1, tn), lambda i, j: (i, 0, j))],
        out_specs=pl.BlockSpec((None, MOD_ROWS, tn), lambda i, j: (i, 0, j)),
        out_shape=jax.ShapeDtypeStruct((depth, MOD_ROWS, n6), F32),
        compiler_params=_cparams(("arbitrary", "arbitrary")),
        name="mod_tables",
    )(cc, ada_w, ada_b.reshape(depth, 1, n6))


def _qkv_kernel(h_ref, g_ref, sh_ref, sc_ref, wqk_ref, wvt_ref, cos_ref, s1_ref, s2_ref,
                qk_ref, vt_ref, *, n_lat, ctx_row):
    b = pl.program_id(0)
    i = pl.program_id(1)
    tt, d = h_ref.shape
    shift = _row_mod(sh_ref, b, ctx_row, i * tt, tt, n_lat)
    scale = _row_mod(sc_ref, b, ctx_row, i * tt, tt, n_lat)
    xn = _ln_mod(h_ref[...], g_ref[...], shift, scale).astype(BF16)
    cw = 2 * HEAD_W
    cos = jnp.concatenate([cos_ref[...]] * 2, axis=1)
    s1 = jnp.concatenate([s1_ref[...]] * 2, axis=1)
    s2 = jnp.concatenate([s2_ref[...]] * 2, axis=1)
    for n in range(2 * d // cw):
        acc = jnp.dot(xn, wqk_ref[:, n * cw:(n + 1) * cw], preferred_element_type=F32)
        r = acc * cos + pltpu.roll(acc, cw - 16, 1) * s1 + pltpu.roll(acc, 16, 1) * s2
        if n * cw < d:
            r = r * ((0.5 * HEAD_W) ** -0.5 * math.log2(math.e))
        qk_ref[:, n * cw:(n + 1) * cw] = r.astype(BF16)
    for n in range(d // cw):
        vt = lax.dot_general(wvt_ref[n * cw:(n + 1) * cw, :], xn, (((1,), (1,)), ((), ())),
                             preferred_element_type=F32)
        vt_ref[n * cw:(n + 1) * cw, :] = vt.astype(BF16)


def _qkv(h, g, mod, wqk, wvt, cos, s1, s2, *, n_lat, tt):
    bsz, t, d = h.shape
    kern = functools.partial(_qkv_kernel, n_lat=n_lat, ctx_row=bsz)
    return pl.pallas_call(
        kern,
        grid=(bsz, t // tt),
        in_specs=[pl.BlockSpec((None, tt, d), lambda b, i: (b, i, 0)),
                  pl.BlockSpec((1, d), lambda b, i: (0, 0)),
                  pl.BlockSpec((MOD_ROWS, d), lambda b, i: (0, 0)),
                  pl.BlockSpec((MOD_ROWS, d), lambda b, i: (0, 1)),
                  pl.BlockSpec((d, 2 * d), lambda b, i: (0, 0)),
                  pl.BlockSpec((d, d), lambda b, i: (0, 0)),
                  pl.BlockSpec((tt, HEAD_W), lambda b, i: (i, 0)),
                  pl.BlockSpec((tt, HEAD_W), lambda b, i: (i, 0)),
                  pl.BlockSpec((tt, HEAD_W), lambda b, i: (i, 0))],
        out_specs=[pl.BlockSpec((None, tt, 2 * d), lambda b, i: (b, i, 0)),
                   pl.BlockSpec((None, None, d, tt), lambda b, i: (b, i, 0, 0))],
        out_shape=[jax.ShapeDtypeStruct((bsz, t, 2 * d), BF16),
                   jax.ShapeDtypeStruct((bsz, t // tt, d, tt), BF16)],
        compiler_params=_cparams(("arbitrary", "arbitrary")),
        name="qkv_rope",
    )(h, g, mod, mod, wqk, wvt, cos, s1, s2)


def _rope_tables(n_lat, n_ctx):
    freqs = HEAD_W // 8
    rows = n_lat // GRID_W
    row = jnp.repeat(jnp.arange(rows), GRID_W).astype(F32)
    col = jnp.tile(jnp.arange(GRID_W), rows).astype(F32)
    inv = ROPE_THETA ** (-(jnp.arange(freqs, dtype=F32) * 2.0) / (2 * freqs))
    lane = jnp.arange(HEAD_W)
    dd = lane % (HEAD_W // 2)
    axis = dd // (2 * freqs)
    half = (dd % (2 * freqs)) // freqs
    f = dd % freqs
    pos = jnp.where(axis[None, :] == 0, row[:, None], col[:, None])
    ang = pos * inv[f][None, :]
    cos = jnp.cos(ang)
    sin = jnp.sin(ang)
    s1 = jnp.where(half[None, :] == 0, -sin, 0.0)
    s2 = jnp.where(half[None, :] == 1, sin, 0.0)
    pad = lambda a, v: jnp.concatenate([a, jnp.full((n_ctx, HEAD_W), v, F32)], axis=0)
    return pad(cos, 1.0), pad(s1, 0.0), pad(s2, 0.0)


def _diff_lambda(lam_ref, lam_init):
    lv = lam_ref[...]
    return (jnp.exp(jnp.sum(lv[0:1, :] * lv[1:2, :], axis=1, keepdims=True))
            - jnp.exp(jnp.sum(lv[2:3, :] * lv[3:4, :], axis=1, keepdims=True)) + lam_init)


def _split_q(q_ref, hh):
    q = q_ref[:, hh * HEAD_W:(hh + 1) * HEAD_W].astype(F32)
    lane = lax.broadcasted_iota(I32, q.shape, 1)
    return jnp.concatenate([jnp.where(lane < HEAD_W // 2, q, 0.0),
                            jnp.where(lane >= HEAD_W // 2, q, 0.0)], axis=0).astype(BF16)


def _attn_finish(o_ref, sg_ref, hh, l8, acc, lam, lam_init):
    tq = o_ref.shape[0]
    r = 1.0 / jnp.sum(l8, axis=0, keepdims=True)
    ot = acc[:, :tq] * r[:, :tq] - lam * (acc[:, tq:] * r[:, tq:])
    msq = jnp.mean(ot * ot, axis=0, keepdims=True)
    ot = (ot * lax.rsqrt(msq + EPS)) * sg_ref[...] * (1.0 - lam_init)
    o_ref[:, hh * HEAD_W:(hh + 1) * HEAD_W] = ot.T.astype(BF16)


def _attn_pipe_kernel(lam_ref, sg_ref, q_ref, k_ref, vt_ref, o_ref, s0_scr, s1_scr, m0_scr, m1_scr,
                      acc_scr, *, lam_init):
    f = pl.program_id(1)
    tq = q_ref.shape[0]
    nh = q_ref.shape[1] // HEAD_W
    ng, _, gk = vt_ref.shape

    @pl.when(f == 0)
    def _():
        s1_scr[...] = jnp.zeros_like(s1_scr)
        m1_scr[...] = jnp.zeros_like(m1_scr)

    def step(sw_scr, mw_scr, sr_scr, mr_scr):
        lam = _diff_lambda(lam_ref, lam_init)
        qbd = [_split_q(q_ref, hh) for hh in range(nh)]
        m_prev = [mr_scr[hh] for hh in range(nh)]
        acc_scr[...] = jnp.zeros_like(acc_scr)

        def body(g, carry):
            m8s, l8s = carry
            base = pl.multiple_of(g * gk, gk)
            new_m8, new_l8 = [], []
            for hh in range(nh):
                cols = slice(hh * HEAD_W, (hh + 1) * HEAD_W)
                s = lax.dot_general(k_ref[pl.ds(base, gk), cols], qbd[hh],
                                    (((1,), (1,)), ((), ())), preferred_element_type=F32)
                sw_scr[hh, pl.ds(base, gk), :] = s
                new_m8.append(jnp.maximum(
                    m8s[hh], jnp.max(s.reshape(gk // SUBLANES, SUBLANES, 2 * tq), axis=0)))
                e = jnp.exp2(sr_scr[hh, pl.ds(base, gk), :] - m_prev[hh])
                new_l8.append(
                    l8s[hh] + jnp.sum(e.reshape(gk // SUBLANES, SUBLANES, 2 * tq), axis=0))
                acc_scr[hh] += jnp.dot(vt_ref[g, cols, :], e.astype(BF16),
                                       preferred_element_type=F32)
            return tuple(new_m8), tuple(new_l8)

        init = (tuple(jnp.full((SUBLANES, 2 * tq), -jnp.inf, F32) for _ in range(nh)),
                tuple(jnp.zeros((SUBLANES, 2 * tq), F32) for _ in range(nh)))
        m8s, l8s = init
        for g in range(ng):
            m8s, l8s = body(g, (m8s, l8s))
        for hh in range(nh):
            mw_scr[hh] = jnp.max(m8s[hh], axis=0, keepdims=True)
            _attn_finish(o_ref, sg_ref, hh, l8s[hh], acc_scr[hh], lam, lam_init)

    @pl.when(lax.rem(f, 2) == 0)
    def _():
        step(s0_scr, m0_scr, s1_scr, m1_scr)

    @pl.when(lax.rem(f, 2) == 1)
    def _():
        step(s1_scr, m1_scr, s0_scr, m0_scr)


def _attention_lat(qk, vt, lamv, sg, *, n_lat, tq, nh, lam_init):
    bsz, t, d2 = qk.shape
    d = d2 // 2
    hw = nh * HEAD_W
    heads = d // hw
    ng, gk = vt.shape[1], vt.shape[3]
    n_tiles = n_lat // tq
    last = heads * n_tiles - 1
    cur = lambda f: jnp.minimum(f, last)
    prev = lambda f: jnp.maximum(f - 1, 0)
    kern = functools.partial(_attn_pipe_kernel, lam_init=lam_init)
    return pl.pallas_call(
        kern,
        grid=(bsz, heads * n_tiles + 1),
        in_specs=[pl.BlockSpec((4, HEAD_W // 2), lambda b, f: (0, 0)),
                  pl.BlockSpec((HEAD_W, 1), lambda b, f: (0, 0)),
                  pl.BlockSpec((None, tq, hw), lambda b, f: (b, cur(f) % n_tiles, cur(f) // n_tiles)),
                  pl.BlockSpec((None, t, hw), lambda b, f: (b, 0, heads + cur(f) // n_tiles)),
                  pl.BlockSpec((None, ng, hw, gk), lambda b, f: (b, 0, prev(f) // n_tiles, 0))],
        out_specs=pl.BlockSpec((None, tq, hw),
                               lambda b, f: (b, prev(f) % n_tiles, prev(f) // n_tiles)),
        out_shape=jax.ShapeDtypeStruct((bsz, n_lat, d), BF16),
        scratch_shapes=[pltpu.VMEM((nh, t, 2 * tq), F32), pltpu.VMEM((nh, t, 2 * tq), F32),
                        pltpu.VMEM((nh, 1, 2 * tq), F32), pltpu.VMEM((nh, 1, 2 * tq), F32),
                        pltpu.VMEM((nh, HEAD_W, 2 * tq), F32)],
        compiler_params=_cparams(("arbitrary", "arbitrary")),
        name="diff_attn_lat",
    )(lamv, sg, qk, qk, vt)


def _attn_kernel(lam_ref, sg_ref, q_ref, k_ref, vt_ref, o_ref, s_scr, *, kc, v_off, lam_init):
    tq = q_ref.shape[0]
    tk = k_ref.shape[0]
    nh = q_ref.shape[1] // HEAD_W
    lam = _diff_lambda(lam_ref, lam_init)

    chunks = range(tk // kc)
    ms = []
    for hh in range(nh):
        qb = _split_q(q_ref, hh)
        m8 = jnp.full((SUBLANES, 2 * tq), -jnp.inf, F32)
        for c in chunks:
            s = lax.dot_general(k_ref[c * kc:(c + 1) * kc, hh * HEAD_W:(hh + 1) * HEAD_W], qb,
                                (((1,), (1,)), ((), ())), preferred_element_type=F32)
            s_scr[hh, c * kc:(c + 1) * kc, :] = s
            m8 = jnp.maximum(m8, jnp.max(s.reshape(kc // SUBLANES, SUBLANES, 2 * tq), axis=0))
        ms.append(jnp.max(m8, axis=0, keepdims=True))
    for hh in range(nh):
        l8 = jnp.zeros((SUBLANES, 2 * tq), F32)
        acc = jnp.zeros((HEAD_W, 2 * tq), F32)
        for c in chunks:
            e = jnp.exp2(s_scr[hh, c * kc:(c + 1) * kc, :] - ms[hh])
            l8 = l8 + jnp.sum(e.reshape(kc // SUBLANES, SUBLANES, 2 * tq), axis=0)
            acc = acc + jnp.dot(
                vt_ref[hh * HEAD_W:(hh + 1) * HEAD_W, v_off + c * kc:v_off + (c + 1) * kc],
                e.astype(BF16), preferred_element_type=F32)
        _attn_finish(o_ref, sg_ref, hh, l8, acc, lam, lam_init)


def _attention_ctx(qk, vt, lamv, sg, *, n_lat, nh, lam_init):
    bsz, t, d2 = qk.shape
    d = d2 // 2
    hw = nh * HEAD_W
    heads = d // hw
    n_ctx = t - n_lat
    gk = vt.shape[3]
    assert n_lat % n_ctx == 0 and n_lat % gk + n_ctx <= gk
    rb = n_lat // n_ctx
    kern = functools.partial(_attn_kernel, kc=n_ctx, v_off=n_lat % gk, lam_init=lam_init)
    return pl.pallas_call(
        kern,
        grid=(bsz, heads),
        in_specs=[pl.BlockSpec((4, HEAD_W // 2), lambda b, h: (0, 0)),
                  pl.BlockSpec((HEAD_W, 1), lambda b, h: (0, 0)),
                  pl.BlockSpec((None, n_ctx, hw), lambda b, h: (b, rb, h)),
                  pl.BlockSpec((None, n_ctx, hw), lambda b, h: (b, rb, heads + h)),
                  pl.BlockSpec((None, None, hw, gk), lambda b, h: (b, n_lat // gk, h, 0))],
        out_specs=pl.BlockSpec((None, n_ctx, hw), lambda b, h: (b, 0, h)),
        out_shape=jax.ShapeDtypeStruct((bsz, n_ctx, d), BF16),
        scratch_shapes=[pltpu.VMEM((nh, n_ctx, 2 * n_ctx), F32)],
        compiler_params=_cparams(("arbitrary", "arbitrary")),
        name="diff_attn_ctx",
    )(lamv, sg, qk, qk, vt)


def _proj_res_kernel(u_ref, w_ref, h_ref, gate_ref, o_ref, *, n_lat, ctx_row, nc):
    b = pl.program_id(0)
    i = pl.program_id(1)
    tt, d = h_ref.shape
    gate = _row_mod(gate_ref, b, ctx_row, i * tt, tt, n_lat)
    u = u_ref[...]
    for n in range(d // nc):
        y = jnp.dot(u, w_ref[:, n * nc:(n + 1) * nc], preferred_element_type=F32)
        o_ref[:, n * nc:(n + 1) * nc] = (h_ref[:, n * nc:(n + 1) * nc]
                                         + gate[:, n * nc:(n + 1) * nc] * y)


def _proj_res(u, w, h, mod, *, gate_chunk, n_rows, n_lat, tt):
    bsz, _, k = u.shape
    d = h.shape[-1]
    kern = functools.partial(_proj_res_kernel, n_lat=n_lat, ctx_row=bsz, nc=256)
    return pl.pallas_call(
        kern,
        grid=(bsz, n_rows // tt),
        in_specs=[pl.BlockSpec((None, tt, k), lambda b, i: (b, i, 0)),
                  pl.BlockSpec((k, d), lambda b, i: (0, 0)),
                  pl.BlockSpec((None, tt, d), lambda b, i: (b, i, 0)),
                  pl.BlockSpec((MOD_ROWS, d), lambda b, i: (0, gate_chunk))],
        out_specs=pl.BlockSpec((None, tt, d), lambda b, i: (b, i, 0)),
        out_shape=jax.ShapeDtypeStruct((bsz, n_rows, d), F32),
        compiler_params=_cparams(("arbitrary", "arbitrary")),
        name="proj_residual",
    )(u, w, h, mod)


def _lru_in_kernel(h_ref, g_ref, sh_ref, sc_ref, w_ref, bias_ref, gy_ref, xr_ref, *,
                   n_lat, ctx_row, nc):
    b = pl.program_id(0)
    i = pl.program_id(1)
    tt, d = h_ref.shape
    shift = _row_mod(sh_ref, b, ctx_row, i * tt, tt, n_lat)
    scale = _row_mod(sc_ref, b, ctx_row, i * tt, tt, n_lat)
    xn = _ln_mod(h_ref[...], g_ref[...], shift, scale).astype(BF16)
    for n in range(d // nc):
        y = jnp.dot(xn, w_ref[:, n * nc:(n + 1) * nc], preferred_element_type=F32)
        y = y + bias_ref[:, n * nc:(n + 1) * nc]
        gy_ref[:, n * nc:(n + 1) * nc] = _gelu_tanh(y).astype(BF16)
    for n in range(d // nc):
        x = jnp.dot(xn, w_ref[:, d + n * nc:d + (n + 1) * nc], preferred_element_type=F32)
        xr_ref[:, n * nc:(n + 1) * nc] = x + bias_ref[:, d + n * nc:d + (n + 1) * nc]


def _lru_in(h, g, mod, w_in, b_in, *, n_lat, tt):
    bsz, t, d = h.shape
    kern = functools.partial(_lru_in_kernel, n_lat=n_lat, ctx_row=bsz, nc=256)
    return pl.pallas_call(
        kern,
        grid=(bsz, t // tt),
        in_specs=[pl.BlockSpec((None, tt, d), lambda b, i: (b, i, 0)),
                  pl.BlockSpec((1, d), lambda b, i: (0, 0)),
                  pl.BlockSpec((MOD_ROWS, d), lambda b, i: (0, 0)),
                  pl.BlockSpec((MOD_ROWS, d), lambda b, i: (0, 1)),
                  pl.BlockSpec((d, 2 * d), lambda b, i: (0, 0)),
                  pl.BlockSpec((1, 2 * d), lambda b, i: (0, 0))],
        out_specs=[pl.BlockSpec((None, tt, d), lambda b, i: (b, i, 0)),
                   pl.BlockSpec((None, tt, d), lambda b, i: (b, i, 0))],
        out_shape=[jax.ShapeDtypeStruct((bsz, t, d), BF16),
                   jax.ShapeDtypeStruct((bsz, t, d), F32)],
        compiler_params=_cparams(("arbitrary", "arbitrary")),
        name="lru_in_proj",
    )(h, g, mod, mod, w_in, b_in)


def _lru_kernel(xr_ref, gy_ref, cw_ref, cb_ref, wg_ref, bg_ref, lam_ref, u_ref,
                xp_scr, hf_scr, hb_scr, *, n_lat):
    t, cw = xr_ref.shape
    n_ctx = t - n_lat
    seg, pitch, nsub = LRU_SEG, LRU_PITCH, SUBLANES
    rc = seg * nsub
    nseg_lat, nseg_ctx = n_lat // seg, n_ctx // seg
    lat0, ctx0 = 1, nseg_lat + 2
    zseg = jnp.zeros((seg, cw), F32)
    for s in (0, nseg_lat + 1, nseg_lat + nseg_ctx + 2):
        xp_scr[s * pitch:s * pitch + seg, :] = zseg
    for s in range(nseg_lat):
        xp_scr[(lat0 + s) * pitch:(lat0 + s) * pitch + seg, :] = xr_ref[s * seg:(s + 1) * seg, :]
    for s in range(nseg_ctx):
        xp_scr[(ctx0 + s) * pitch:(ctx0 + s) * pitch + seg, :] = (
            xr_ref[n_lat + s * seg:n_lat + (s + 1) * seg, :])
    lam = lam_ref[...]
    sp = jnp.maximum(-lam, 0.0) + jnp.log(1.0 + jnp.exp(-jnp.abs(lam)))
    w = cw_ref[...]
    cb = cb_ref[...]

    def rows8(ref, start):
        return ref[pl.ds(start, nsub, stride=pitch), :]

    def chunk(pb, carry, d, reverse, out_scr):
        x = [rows8(xp_scr, pb + g) for g in range(seg)]
        xm1 = [rows8(xp_scr, pb - (pitch - seg) - 1)] + x[:-1]
        xm2 = [rows8(xp_scr, pb - (pitch - seg) - 2), xm1[0]] + x[:-2]
        xp1 = x[1:] + [rows8(xp_scr, pb + pitch)]
        xc = jnp.concatenate(
            [cb + w[0:1, :] * xm2[g] + w[1:2, :] * xm1[g] + w[2:3, :] * x[g] + w[3:4, :] * xp1[g]
             for g in range(seg)], axis=0)
        gpre = jnp.dot(xc.astype(BF16), wg_ref[d, 0], preferred_element_type=F32) + bg_ref[d, 0]
        r = _sigmoid(gpre[:, :LRU_BW])
        ig = _sigmoid(gpre[:, LRU_BW:])
        a = jnp.exp((-LRU_C * sp[d:d + 1, :]) * r)
        bt = jnp.sqrt(1.0 - a * a) * (ig * xc)
        a3 = a.reshape(seg, nsub, cw)
        b3 = bt.reshape(seg, nsub, cw)
        h = jnp.zeros((nsub, cw), F32)
        p = jnp.ones((nsub, cw), F32)
        hl, pl_ = [None] * seg, [None] * seg
        for g in (range(seg - 1, -1, -1) if reverse else range(seg)):
            h = a3[g] * h + b3[g]
            p = a3[g] * p
            hl[g], pl_[g] = h, p
        entry = [None] * nsub
        for j in (range(nsub - 1, -1, -1) if reverse else range(nsub)):
            entry[j] = carry
            carry = h[j:j + 1, :] + p[j:j + 1, :] * carry
        h_in = jnp.concatenate(entry, axis=0)
        for g in range(seg):
            out_scr[pl.ds(pb + g, nsub, stride=pitch), :] = hl[g] + pl_[g] * h_in
        return carry

    lat_pb = lambda c: lat0 * pitch + c * (nsub * pitch)
    ctx_pb = lambda c: (ctx0 + c * nsub) * pitch
    n_lat_chunks, n_ctx_chunks = n_lat // rc, n_ctx // rc
    cf = cb_ = jnp.zeros((1, cw), F32)
    for c in range(n_ctx_chunks):
        cf = chunk(ctx_pb(c), cf, 0, False, hf_scr)
        cb_ = chunk(ctx_pb(n_ctx_chunks - 1 - c), cb_, 1, True, hb_scr)

    def both(c, carries):
        return (chunk(lat_pb(c), carries[0], 0, False, hf_scr),
                chunk(lat_pb(n_lat_chunks - 1 - c), carries[1], 1, True, hb_scr))

    lax.fori_loop(0, n_lat_chunks, both, (cf, cb_))
    for s in range(nseg_lat):
        rows = slice((lat0 + s) * pitch, (lat0 + s) * pitch + seg)
        u_ref[s * seg:(s + 1) * seg, :] = (gy_ref[s * seg:(s + 1) * seg, :].astype(F32)
                                           * (hf_scr[rows, :] + hb_scr[rows, :])).astype(BF16)


def _lru_core(xr, gy, conv_w, conv_b, w_gates, b_gates, lam, *, n_lat):
    bsz, t, d = xr.shape
    cw = LRU_BW
    rows = ((t // LRU_SEG) + 3) * LRU_PITCH
    kern = functools.partial(_lru_kernel, n_lat=n_lat)
    return pl.pallas_call(
        kern,
        grid=(bsz, d // cw),
        in_specs=[pl.BlockSpec((None, t, cw), lambda b, k: (b, 0, k)),
                  pl.BlockSpec((None, n_lat, cw), lambda b, k: (b, 0, k)),
                  pl.BlockSpec((CONV_W, cw), lambda b, k: (0, k)),
                  pl.BlockSpec((1, cw), lambda b, k: (0, k)),
                  pl.BlockSpec((2, 1, LRU_BW, 2 * LRU_BW), lambda b, k: (0, k, 0, 0)),
                  pl.BlockSpec((2, 1, 1, 2 * LRU_BW), lambda b, k: (0, k, 0, 0)),
                  pl.BlockSpec((2, cw), lambda b, k: (0, k))],
        out_specs=pl.BlockSpec((None, n_lat, cw), lambda b, k: (b, 0, k)),
        out_shape=jax.ShapeDtypeStruct((bsz, n_lat, d), BF16),
        scratch_shapes=[pltpu.VMEM((rows, cw), F32)] * 3,
        compiler_params=_cparams(("arbitrary", "arbitrary")),
        name="lru_core",
    )(xr, gy, conv_w, conv_b, w_gates, b_gates, lam)


def _norm_probs_kernel(h_ref, g_ref, sh_ref, sc_ref, wr_ref, m_ref, p_ref, *,
                       n_lat, ctx_row, n_exp):
    b = pl.program_id(0)
    i = pl.program_id(1)
    tt, d = h_ref.shape
    shift = _row_mod(sh_ref, b, ctx_row, i * tt, tt, n_lat)
    scale = _row_mod(sc_ref, b, ctx_row, i * tt, tt, n_lat)
    m = _ln_mod(h_ref[...], g_ref[...], shift, scale)
    m_hi = m.astype(BF16)
    m_ref[...] = m_hi
    m_lo = (m - m_hi.astype(F32)).astype(BF16)
    wr = wr_ref[...]
    w_hi = wr.astype(BF16)
    w_lo = (wr - w_hi.astype(F32)).astype(BF16)
    logits = (jnp.dot(m_hi, w_hi, preferred_element_type=F32)
              + jnp.dot(m_lo, w_hi, preferred_element_type=F32)
              + jnp.dot(m_hi, w_lo, preferred_element_type=F32))
    lane = lax.broadcasted_iota(I32, logits.shape, 1)
    lg = jnp.where(lane < n_exp, logits, -jnp.inf)
    ex = jnp.exp(lg - jnp.max(lg, axis=1, keepdims=True))
    p_ref[...] = ex / jnp.sum(ex, axis=1, keepdims=True)


def _norm_probs(h, g, mod, wr, *, n_lat, n_exp, tt):
    bsz, t, d = h.shape
    kern = functools.partial(_norm_probs_kernel, n_lat=n_lat, ctx_row=bsz, n_exp=n_exp)
    return pl.pallas_call(
        kern,
        grid=(bsz, t // tt),
        in_specs=[pl.BlockSpec((None, tt, d), lambda b, i: (b, i, 0)),
                  pl.BlockSpec((1, d), lambda b, i: (0, 0)),
                  pl.BlockSpec((MOD_ROWS, d), lambda b, i: (0, 3)),
                  pl.BlockSpec((MOD_ROWS, d), lambda b, i: (0, 4)),
                  pl.BlockSpec((d, LANES), lambda b, i: (0, 0))],
        out_specs=[pl.BlockSpec((None, tt, d), lambda b, i: (b, i, 0)),
                   pl.BlockSpec((None, tt, LANES), lambda b, i: (b, i, 0))],
        out_shape=[jax.ShapeDtypeStruct((bsz, t, d), BF16),
                   jax.ShapeDtypeStruct((bsz, t, LANES), F32)],
        compiler_params=_cparams(("arbitrary", "arbitrary")),
        name="moe_norm_probs",
    )(h, g, mod, mod, wr)


def _select(p, cap, n_exp):
    n_tok = p.shape[0]
    n_rows = -(-n_exp // SUBLANES) * SUBLANES
    pt = p.T[0:n_rows, :]

    def body(i, thr):
        cand = thr | jnp.left_shift(jnp.int32(1), 29 - i)
        hit = jnp.where(pt >= lax.bitcast_convert_type(cand, F32), 1.0, 0.0)
        return jnp.where(jnp.sum(hit, axis=1, keepdims=True) >= cap, cand, thr)

    thr_col = lax.fori_loop(0, 30, body, jnp.zeros((n_rows, 1), I32))
    diag = (lax.broadcasted_iota(I32, (n_rows, LANES), 0)
            == lax.broadcasted_iota(I32, (n_rows, LANES), 1))
    thr = jnp.sum(jnp.where(diag, jnp.broadcast_to(thr_col, (n_rows, LANES)), 0),
                  axis=0, keepdims=True)
    gt = p >= pltpu.bitcast(thr + 1, F32)
    eq = (p >= pltpu.bitcast(thr, F32)) & jnp.logical_not(gt)
    blk = PREFIX_BLOCK
    ltri = jnp.where(lax.broadcasted_iota(I32, (blk, blk), 1)
                     < lax.broadcasted_iota(I32, (blk, blk), 0), 1.0, 0.0).astype(BF16)
    masks = jnp.concatenate([jnp.where(gt, 1.0, 0.0), jnp.where(eq, 1.0, 0.0)], axis=1)
    off = jnp.zeros((1, 2 * LANES), F32)
    pres = []
    for i in range(n_tok // blk):
        mb = masks[i * blk:(i + 1) * blk, :]
        pre = jnp.dot(ltri, mb.astype(BF16), preferred_element_type=F32) + off
        pres.append(pre)
        off = pre[blk - 1:blk, :] + mb[blk - 1:blk, :]
    pre = jnp.concatenate(pres, axis=0)
    pre_gt, pre_eq = pre[:, :LANES], pre[:, LANES:]
    need = cap - off[:, :LANES]
    sel = gt | (eq & (pre_eq < need))
    slot = pre_gt + jnp.minimum(pre_eq, need)
    bounds = jnp.concatenate([slot[i * blk:i * blk + 1, :] for i in range(n_tok // blk)]
                             + [jnp.full((1, LANES), float(cap), F32)], axis=0)
    return jnp.where(sel, slot, -1.0), jnp.where(sel, p, 0.0), bounds


def _select_kernel(p_ref, slot_ref, gate_ref, slot_t_ref, bounds_ref, *, n_lat, n_exp):
    t = p_ref.shape[0]
    n_ctx = t - n_lat
    slot, gate, bounds = _select(p_ref[0:n_lat, :], EC_CAPACITY * n_lat // n_exp, n_exp)
    slot_ref[0:n_lat, :] = slot
    gate_ref[0:n_lat, :] = gate
    slot_t_ref[:, 0:n_lat] = slot.T
    bounds_ref[...] = jnp.zeros_like(bounds_ref)
    bounds_ref[0:bounds.shape[0], :] = bounds
    if n_ctx:
        slot, gate, _ = _select(p_ref[n_lat:t, :], EC_CAPACITY * n_ctx // n_exp, n_exp)
        slot_ref[n_lat:t, :] = slot
        gate_ref[n_lat:t, :] = gate
        slot_t_ref[:, n_lat:t] = slot.T


def _select_tokens(p, *, n_lat, n_exp):
    bsz, t, _ = p.shape
    nb_rows = -(-(n_lat // PREFIX_BLOCK + 1) // SUBLANES) * SUBLANES
    kern = functools.partial(_select_kernel, n_lat=n_lat, n_exp=n_exp)
    return pl.pallas_call(
        kern,
        grid=(bsz,),
        in_specs=[pl.BlockSpec((None, t, LANES), lambda b: (b, 0, 0))],
        out_specs=[pl.BlockSpec((None, t, LANES), lambda b: (b, 0, 0)),
                   pl.BlockSpec((None, t, LANES), lambda b: (b, 0, 0)),
                   pl.BlockSpec((None, LANES, t), lambda b: (b, 0, 0)),
                   pl.BlockSpec((None, nb_rows, LANES), lambda b: (b, 0, 0))],
        out_shape=[jax.ShapeDtypeStruct((bsz, t, LANES), F32),
                   jax.ShapeDtypeStruct((bsz, t, LANES), F32),
                   jax.ShapeDtypeStruct((bsz, LANES, t), F32),
                   jax.ShapeDtypeStruct((bsz, nb_rows, LANES), F32)],
        compiler_params=_cparams(("arbitrary",)),
        name="moe_select",
    )(p)


def _gather_kernel(m_ref, slot_t_ref, xg_ref, *, n_lat, cap_lat, cap_ctx):
    j = pl.program_id(1)
    eg = xg_ref.shape[0]
    t = m_ref.shape[0]
    for ee in range(eg):
        row = slot_t_ref[pl.ds(j * eg + ee, 1), :]
        sl = lax.broadcasted_iota(I32, (cap_lat, 1), 0).astype(F32)
        p = jnp.where(row[:, 0:n_lat] == sl, 1.0, 0.0).astype(BF16)
        xg_ref[ee, 0:cap_lat, :] = jnp.dot(p, m_ref[0:n_lat, :],
                                           preferred_element_type=F32).astype(BF16)
        if cap_ctx:
            sc = lax.broadcasted_iota(I32, (cap_ctx, 1), 0).astype(F32)
            pc = jnp.where(row[:, n_lat:t] == sc, 1.0, 0.0).astype(BF16)
            xg_ref[ee, cap_lat:cap_lat + cap_ctx, :] = jnp.dot(
                pc, m_ref[n_lat:t, :], preferred_element_type=F32).astype(BF16)


def _gather(m, slot_t, *, n_lat, n_exp, eg):
    bsz, t, d = m.shape
    cap_lat = EC_CAPACITY * n_lat // n_exp
    cap_ctx = EC_CAPACITY * (t - n_lat) // n_exp
    r = cap_lat + cap_ctx
    kern = functools.partial(_gather_kernel, n_lat=n_lat, cap_lat=cap_lat, cap_ctx=cap_ctx)
    return pl.pallas_call(
        kern,
        grid=(bsz, n_exp // eg),
        in_specs=[pl.BlockSpec((None, t, d), lambda b, j: (b, 0, 0)),
                  pl.BlockSpec((None, LANES, t), lambda b, j: (b, 0, 0))],
        out_specs=pl.BlockSpec((eg, None, r, d), lambda b, j: (j, b, 0, 0)),
        out_shape=jax.ShapeDtypeStruct((n_exp, bsz, r, d), BF16),
        compiler_params=_cparams(("arbitrary", "arbitrary")),
        name="moe_gather",
    )(m, slot_t)


def _ffn_kernel(x_ref, wg_ref, wu_ref, wd_ref, y_ref, acc_scr, *, rc):
    j = pl.program_id(1)
    nj = pl.num_programs(1)
    rows = x_ref.shape[0]

    @pl.when(j == 0)
    def _():
        acc_scr[...] = jnp.zeros_like(acc_scr)

    wg = wg_ref[...].astype(BF16)
    wu = wu_ref[...].astype(BF16)
    wd = wd_ref[...].astype(BF16)

    for c in range(rows // rc):
        x = x_ref[c * rc:(c + 1) * rc, :]
        g = jnp.dot(x, wg, preferred_element_type=F32)
        u = jnp.dot(x, wu, preferred_element_type=F32)
        hid = ((g * _sigmoid(g)) * u).astype(BF16)
        acc_scr[c * rc:(c + 1) * rc, :] += jnp.dot(hid, wd, preferred_element_type=F32)

    @pl.when(j == nj - 1)
    def _():
        y_ref[...] = acc_scr[...].astype(BF16)


def _ffn(xg, w_gate_up, w_down, *, layer, fc, rc):
    n_exp, rows, d = xg.shape
    f = w_down.shape[2]
    nj = f // fc
    kern = functools.partial(_ffn_kernel, rc=rc)
    return pl.pallas_call(
        kern,
        grid=(n_exp, nj),
        in_specs=[pl.BlockSpec((None, rows, d), lambda e, j: (e, 0, 0)),
                  pl.BlockSpec((None, None, d, fc), lambda e, j: (layer, e, 0, j)),
                  pl.BlockSpec((None, None, d, fc), lambda e, j: (layer, e, 0, nj + j)),
                  pl.BlockSpec((None, None, fc, d), lambda e, j: (layer, e, j, 0))],
        out_specs=pl.BlockSpec((None, rows, d), lambda e, j: (e, 0, 0)),
        out_shape=jax.ShapeDtypeStruct((n_exp, rows, d), BF16),
        scratch_shapes=[pltpu.VMEM((rows, d), F32)],
        compiler_params=_cparams(("arbitrary", "arbitrary")),
        name="moe_ffn",
    )(xg, w_gate_up, w_gate_up, w_down)


def _combine_kernel(bounds_ref, y_ref, slot_ref, gate_ref, h_ref, g2_ref, fg_ref, o_ref, *,
                    n_lat, ctx_row, cap_lat, cap_ctx, final_norm):
    b = pl.program_id(0)
    i = pl.program_id(1)
    tt, d = h_ref.shape
    n_exp = y_ref.shape[0]
    n_lat_tiles = n_lat // tt

    def finish(acc):
        gate2 = _row_mod(g2_ref, b, ctx_row, i * tt, tt, n_lat)
        out = h_ref[...] + gate2 * acc
        if final_norm:
            ms = jnp.mean(out * out, axis=-1, keepdims=True)
            out = (out * lax.rsqrt(ms + EPS)) * fg_ref[...]
        o_ref[...] = out

    win = min(cap_lat, LANES)
    base = (b * (n_lat_tiles + 1) + jnp.minimum(i, n_lat_tiles - 1)) * n_exp
    starts = []
    fits = None
    for e in range(n_exp):
        lo = bounds_ref[base + e]
        hi = bounds_ref[base + n_exp + e]
        st = jnp.minimum(lax.shift_left(lax.shift_right_logical(lo, 4), 4), cap_lat - win)
        starts.append(pl.multiple_of(st, 16))
        ok = hi - st <= win
        fits = ok if fits is None else jnp.logical_and(fits, ok)
    is_lat = i < n_lat_tiles

    @pl.when(jnp.logical_and(is_lat, fits))
    def _():
        slot = slot_ref[...]
        gate = gate_ref[...]
        iota_w = lax.broadcasted_iota(I32, (1, win), 1).astype(F32)
        acc = jnp.zeros((tt, d), F32)
        for e0 in range(0, n_exp, 2):
            pts, ys = [], []
            for e in range(e0, min(e0 + 2, n_exp)):
                rel = slot[:, e:e + 1] - starts[e].astype(F32)
                pts.append(jnp.where(rel == iota_w, gate[:, e:e + 1], 0.0))
                ys.append(y_ref[e, pl.ds(starts[e], win), :])
            acc = acc + jnp.dot(jnp.concatenate(pts, axis=1).astype(BF16),
                                jnp.concatenate(ys, axis=0), preferred_element_type=F32)
        finish(acc)

    @pl.when(jnp.logical_and(is_lat, jnp.logical_not(fits)))
    def _():
        slot = slot_ref[...]
        gate = gate_ref[...]
        iota_c = lax.broadcasted_iota(I32, (1, cap_lat), 1).astype(F32)
        acc = jnp.zeros((tt, d), F32)
        for e in range(n_exp):
            pt = jnp.where(slot[:, e:e + 1] == iota_c, gate[:, e:e + 1], 0.0).astype(BF16)
            acc = acc + jnp.dot(pt, y_ref[e, 0:cap_lat, :], preferred_element_type=F32)
        finish(acc)

    if cap_ctx:
        @pl.when(i >= n_lat_tiles)
        def _():
            slot = slot_ref[...]
            gate = gate_ref[...]
            iota_c = lax.broadcasted_iota(I32, (1, n_exp * cap_ctx), 1).astype(F32)
            pt = jnp.zeros((tt, n_exp * cap_ctx), F32)
            for e in range(n_exp):
                pt = pt + jnp.where(slot[:, e:e + 1] + float(e * cap_ctx) == iota_c,
                                    gate[:, e:e + 1], 0.0)
            yc = y_ref[:, cap_lat:cap_lat + cap_ctx, :].reshape(n_exp * cap_ctx, d)
            finish(jnp.dot(pt.astype(BF16), yc, preferred_element_type=F32))


def _combine(y, slot, gate, bounds, h, mod, final_g, *, n_lat, n_out, final_norm):
    n_exp, bsz, r, d = y.shape
    t = slot.shape[1]
    tt = PREFIX_BLOCK
    cap_lat = EC_CAPACITY * n_lat // n_exp
    cap_ctx = EC_CAPACITY * (t - n_lat) // n_exp
    table = bounds[:, :n_lat // tt + 1, :n_exp].astype(I32).reshape(-1)
    kern = functools.partial(_combine_kernel, n_lat=n_lat, ctx_row=bsz, cap_lat=cap_lat,
                             cap_ctx=cap_ctx, final_norm=final_norm)
    grid_spec = pltpu.PrefetchScalarGridSpec(
        num_scalar_prefetch=1,
        grid=(bsz, n_out // tt),
        in_specs=[pl.BlockSpec((n_exp, None, r, d), lambda b, i, tbl: (0, b, 0, 0)),
                  pl.BlockSpec((None, tt, LANES), lambda b, i, tbl: (b, i, 0)),
                  pl.BlockSpec((None, tt, LANES), lambda b, i, tbl: (b, i, 0)),
                  pl.BlockSpec((None, tt, d), lambda b, i, tbl: (b, i, 0)),
                  pl.BlockSpec((MOD_ROWS, d), lambda b, i, tbl: (0, 5)),
                  pl.BlockSpec((1, d), lambda b, i, tbl: (0, 0))],
        out_specs=pl.BlockSpec((None, tt, d), lambda b, i, tbl: (b, i, 0)))
    return pl.pallas_call(
        kern,
        grid_spec=grid_spec,
        out_shape=jax.ShapeDtypeStruct((bsz, n_out, d), F32),
        compiler_params=_cparams(("arbitrary", "arbitrary")),
        name="moe_combine",
    )(table, y, slot, gate, h, mod, final_g)


def _moe(h, g2, mod, w_router, w_gate_up, w_down, final_g, *, layer, n_lat, final_norm):
    bsz, t, d = h.shape
    n_exp = w_router.shape[-1]
    wr = jnp.pad(w_router, ((0, 0), (0, LANES - n_exp)))
    m, p = _norm_probs(h, g2, mod, wr, n_lat=n_lat, n_exp=n_exp, tt=_pick(t, (768, 512, 256)))
    slot, gate, slot_t, bounds = _select_tokens(p, n_lat=n_lat, n_exp=n_exp)
    xg = _gather(m, slot_t, n_lat=n_lat, n_exp=n_exp, eg=4)
    r = xg.shape[2]
    y = _ffn(xg.reshape(n_exp, bsz * r, d), w_gate_up, w_down, layer=layer, fc=256,
             rc=_pick(bsz * r, (768, 1024, 512, 256, 128, 64, 32, 16)))
    return _combine(y.reshape(n_exp, bsz, r, d), slot, gate, bounds, h, mod, final_g,
                    n_lat=n_lat, n_out=t, final_norm=final_norm)


def kernel(x, c, ctx, c_ctx, ada_w, ada_b, norm1_g, norm2_g, final_g, attn_w_qkv, attn_lq1, attn_lk1, attn_lq2, attn_lk2, attn_subln_g, attn_w_o, lru_w_in, lru_b_in, lru_conv_w, lru_conv_b, lru_w_gates, lru_b_gates, lru_lambda, lru_w_out, moe_w_router, moe_w_gate_up, moe_w_down):
    bsz, n_lat, d = x.shape
    n_ctx = ctx.shape[1]
    depth = ada_w.shape[0]
    assert bsz < MOD_ROWS and d % (2 * HEAD_W) == 0 and n_lat % GRID_W == 0

    cc = jnp.concatenate([c, c_ctx[None, :], jnp.zeros((MOD_ROWS - bsz - 1, d), F32)], axis=0)
    mods = _mod_tables(cc, ada_w, ada_b)
    h = jnp.concatenate([x, ctx], axis=1)
    row = lambda v: v.reshape(1, -1)

    for i in range(depth):
        last = i == depth - 1
        mod = mods[i]
        j = i // N_MIXERS
        if i % N_MIXERS == 0:
            lam_init = 0.8 - 0.6 * math.exp(-0.3 * i)
            w = attn_w_qkv[j]
            wqk = w[:, :2 * d].astype(BF16)
            wvt = w[:, 2 * d:].T.astype(BF16)
            cos, s1, s2 = _rope_tables(n_lat, n_ctx)
            t_all = n_lat + n_ctx
            qk, vt = _qkv(h, row(norm1_g[i]), mod, wqk, wvt, cos, s1, s2, n_lat=n_lat,
                          tt=_pick(t_all, (768, 256)))
            lamv = jnp.stack([attn_lq1[j], attn_lk1[j], attn_lq2[j], attn_lk2[j]], axis=0)
            sg = attn_subln_g[j].reshape(HEAD_W, 1)
            n_rows = n_lat if last else n_lat + n_ctx
            o = _attention_lat(qk, vt, lamv, sg, n_lat=n_lat, tq=256, nh=2, lam_init=lam_init)
            if not last:
                o_ctx = _attention_ctx(qk, vt, lamv, sg, n_lat=n_lat, nh=2, lam_init=lam_init)
                o = jnp.concatenate([o, o_ctx], axis=1)
            h = _proj_res(o, attn_w_o[j].astype(BF16), h, mod, gate_chunk=2, n_rows=n_rows,
                          n_lat=n_lat, tt=_pick(n_rows, (768, 512, 256)))
        else:
            gy, xr = _lru_in(h, row(norm1_g[i]), mod, lru_w_in[j].astype(BF16),
                             row(lru_b_in[j]), n_lat=n_lat, tt=_pick(n_lat + n_ctx, (768, 256)))
            nblk = d // LRU_BW
            u = _lru_core(xr, gy, lru_conv_w[j], row(lru_conv_b[j]),
                          lru_w_gates[j].astype(BF16),
                          lru_b_gates[j].reshape(2, nblk, 1, 2 * LRU_BW), lru_lambda[j],
                          n_lat=n_lat)
            n_rows = n_lat if last else n_lat + n_ctx
            assert last, "context output of the recurrent mixer is only needed in non-final layers"
            h = _proj_res(u, lru_w_out[j].astype(BF16), h, mod, gate_chunk=2, n_rows=n_rows,
                          n_lat=n_lat, tt=_pick(n_rows, (1024, 512, 256)))
        h = _moe(h, row(norm2_g[i]), mod, moe_w_router[i], moe_w_gate_up, moe_w_down,
                 row(final_g), layer=i, n_lat=n_lat, final_norm=last)
    return h
```

```python
import functools
import math

import jax
import jax.numpy as jnp
from jax import lax
from jax.experimental import pallas as pl
from jax.experimental.pallas import tpu as pltpu

F32 = jnp.float32
BF16 = jnp.bfloat16
I32 = jnp.int32

EPS = 1e-6
GRID_W = 64
ROPE_THETA = 10000.0
LRU_C = 8.0
EC_CAPACITY = 2
N_MIXERS = 2
HEAD_W = 128
LRU_BW = 128
LRU_SEG = 32
LRU_PITCH = 40
CONV_W = 4
MOD_ROWS = 16
PREFIX_BLOCK = 256
LANES = 128
SUBLANES = 8
VMEM_LIMIT = 56 * 2**20


def _pick(n, cands):
    return next(c for c in cands if n % c == 0)


def _cparams(sem, flags=None):
    return pltpu.CompilerParams(dimension_semantics=sem, vmem_limit_bytes=VMEM_LIMIT, flags=flags)


def _sigmoid(x):
    return 0.5 * jnp.tanh(0.5 * x) + 0.5


def _gelu_tanh(x):
    return 0.5 * x * (1.0 + jnp.tanh(math.sqrt(2.0 / math.pi) * (x + 0.044715 * (x * x * x))))


def _row_mod(mod_ref, b, ctx_row, row0, nrows, n_lat):
    mb = mod_ref[pl.ds(b, 1), :]
    mc = mod_ref[ctx_row:ctx_row + 1, :]
    rows = row0 + lax.broadcasted_iota(I32, (nrows, 1), 0)
    return jnp.where(rows < n_lat, mb, mc)


def _ln_mod(x, g, shift, scale):
    ms = jnp.mean(x * x, axis=-1, keepdims=True)
    y = (x * lax.rsqrt(ms + EPS)) * g
    return y * (1.0 + scale) + shift


def _mod_kernel(c_ref, w_ref, b_ref, o_ref):
    c = c_ref[...]
    s = c * _sigmoid(c)
    o_ref[...] = jnp.dot(s, w_ref[...], preferred_element_type=F32,
                         precision=lax.Precision.HIGHEST) + b_ref[...]


def _mod_tables(cc, ada_w, ada_b):
    depth, d, n6 = ada_w.shape
    tn = _pick(n6, (1536, 768, 384))
    return pl.pallas_call(
        _mod_kernel,
        grid=(depth, n6 // tn),
        in_specs=[pl.BlockSpec((MOD_ROWS, d), lambda i, j: (0, 0)),
                  pl.BlockSpec((None, d, tn), lambda i, j: (i, 0, j)),
                  pl.BlockSpec((None, 1, tn), lambda i, j: (i, 0, j))],
        out_specs=pl.BlockSpec((None, MOD_ROWS, tn), lambda i, j: (i, 0, j)),
        out_shape=jax.ShapeDtypeStruct((depth, MOD_ROWS, n6), F32),
        compiler_params=_cparams(("arbitrary", "arbitrary")),
        name="mod_tables",
    )(cc, ada_w, ada_b.reshape(depth, 1, n6))


def _qkv_kernel(h_ref, g_ref, sh_ref, sc_ref, wqk_ref, wvt_ref, cos_ref, s1_ref, s2_ref,
                qk_ref, vt_ref, *, n_lat, ctx_row):
    b = pl.program_id(0)
    i = pl.program_id(1)
    tt, d = h_ref.shape
    shift = _row_mod(sh_ref, b, ctx_row, i * tt, tt, n_lat)
    scale = _row_mod(sc_ref, b, ctx_row, i * tt, tt, n_lat)
    xn = _ln_mod(h_ref[...], g_ref[...], shift, scale).astype(BF16)
    cw = 2 * HEAD_W
    cos = jnp.concatenate([cos_ref[...]] * 2, axis=1)
    s1 = jnp.concatenate([s1_ref[...]] * 2, axis=1)
    s2 = jnp.concatenate([s2_ref[...]] * 2, axis=1)
    for n in range(2 * d // cw):
        acc = jnp.dot(xn, wqk_ref[:, n * cw:(n + 1) * cw], preferred_element_type=F32)
        r = acc * cos + pltpu.roll(acc, cw - 16, 1) * s1 + pltpu.roll(acc, 16, 1) * s2
        if n * cw < d:
            r = r * ((0.5 * HEAD_W) ** -0.5 * math.log2(math.e))
        qk_ref[:, n * cw:(n + 1) * cw] = r.astype(BF16)
    for n in range(d // cw):
        vt = lax.dot_general(wvt_ref[n * cw:(n + 1) * cw, :], xn, (((1,), (1,)), ((), ())),
                             preferred_element_type=F32).astype(BF16)
        gk = vt_ref.shape[2]
        for s in range(vt_ref.shape[0]):
            vt_ref[s, n * cw:(n + 1) * cw, :] = vt[:, s * gk:(s + 1) * gk]


def _qkv(h, g, mod, wqk, wvt, cos, s1, s2, *, n_lat, tt, vs):
    bsz, t, d = h.shape
    gk = tt // vs
    kern = functools.partial(_qkv_kernel, n_lat=n_lat, ctx_row=bsz)
    return pl.pallas_call(
        kern,
        grid=(bsz, t // tt),
        in_specs=[pl.BlockSpec((None, tt, d), lambda b, i: (b, i, 0)),
                  pl.BlockSpec((1, d), lambda b, i: (0, 0)),
                  pl.BlockSpec((MOD_ROWS, d), lambda b, i: (0, 0)),
                  pl.BlockSpec((MOD_ROWS, d), lambda b, i: (0, 1)),
                  pl.BlockSpec((d, 2 * d), lambda b, i: (0, 0)),
                  pl.BlockSpec((d, d), lambda b, i: (0, 0)),
                  pl.BlockSpec((tt, HEAD_W), lambda b, i: (i, 0)),
                  pl.BlockSpec((tt, HEAD_W), lambda b, i: (i, 0)),
                  pl.BlockSpec((tt, HEAD_W), lambda b, i: (i, 0))],
        out_specs=[pl.BlockSpec((None, tt, 2 * d), lambda b, i: (b, i, 0)),
                   pl.BlockSpec((None, vs, d, gk), lambda b, i: (b, i, 0, 0))],
        out_shape=[jax.ShapeDtypeStruct((bsz, t, 2 * d), BF16),
                   jax.ShapeDtypeStruct((bsz, t // gk, d, gk), BF16)],
        compiler_params=_cparams(("arbitrary", "arbitrary")),
        name="qkv_rope",
    )(h, g, mod, mod, wqk, wvt, cos, s1, s2)


def _rope_tables(n_lat, n_ctx):
    freqs = HEAD_W // 8
    rows = n_lat // GRID_W
    row = jnp.repeat(jnp.arange(rows), GRID_W).astype(F32)
    col = jnp.tile(jnp.arange(GRID_W), rows).astype(F32)
    inv = ROPE_THETA ** (-(jnp.arange(freqs, dtype=F32) * 2.0) / (2 * freqs))
    lane = jnp.arange(HEAD_W)
    dd = lane % (HEAD_W // 2)
    axis = dd // (2 * freqs)
    half = (dd % (2 * freqs)) // freqs
    f = dd % freqs
    pos = jnp.where(axis[None, :] == 0, row[:, None], col[:, None])
    ang = pos * inv[f][None, :]
    cos = jnp.cos(ang)
    sin = jnp.sin(ang)
    s1 = jnp.where(half[None, :] == 0, -sin, 0.0)
    s2 = jnp.where(half[None, :] == 1, sin, 0.0)
    pad = lambda a, v: jnp.concatenate([a, jnp.full((n_ctx, HEAD_W), v, F32)], axis=0)
    return pad(cos, 1.0), pad(s1, 0.0), pad(s2, 0.0)


def _diff_lambda(lam_ref, lam_init):
    lv = lam_ref[...]
    return (jnp.exp(jnp.sum(lv[0:1, :] * lv[1:2, :], axis=1, keepdims=True))
            - jnp.exp(jnp.sum(lv[2:3, :] * lv[3:4, :], axis=1, keepdims=True)) + lam_init)


def _split_q(q_ref, hh):
    q = q_ref[:, hh * HEAD_W:(hh + 1) * HEAD_W].astype(F32)
    lane = lax.broadcasted_iota(I32, q.shape, 1)
    return jnp.concatenate([jnp.where(lane < HEAD_W // 2, q, 0.0),
                            jnp.where(lane >= HEAD_W // 2, q, 0.0)], axis=0).astype(BF16)


def _attn_finish(o_ref, sg_ref, hh, l8, acc, lam, lam_init):
    tq = o_ref.shape[0]
    r = 1.0 / jnp.sum(l8, axis=0, keepdims=True)
    ot = acc[:, :tq] * r[:, :tq] - lam * (acc[:, tq:] * r[:, tq:])
    msq = jnp.mean(ot * ot, axis=0, keepdims=True)
    ot = (ot * lax.rsqrt(msq + EPS)) * sg_ref[...] * (1.0 - lam_init)
    o_ref[:, hh * HEAD_W:(hh + 1) * HEAD_W] = ot.T.astype(BF16)


def _attn_pipe_kernel(lam_ref, sg_ref, q_ref, k_ref, vt_ref, o_ref, s0_scr, s1_scr, m0_scr, m1_scr,
                      acc_scr, *, lam_init):
    f = pl.program_id(1)
    tq = q_ref.shape[0]
    nh = q_ref.shape[1] // HEAD_W
    ng, _, gk = vt_ref.shape

    @pl.when(f == 0)
    def _():
        s1_scr[...] = jnp.zeros_like(s1_scr)
        m1_scr[...] = jnp.zeros_like(m1_scr)

    def step(sw_scr, mw_scr, sr_scr, mr_scr):
        lam = _diff_lambda(lam_ref, lam_init)
        qbd = [_split_q(q_ref, hh) for hh in range(nh)]
        m_prev = [mr_scr[hh] for hh in range(nh)]
        acc_scr[...] = jnp.zeros_like(acc_scr)

        def body(g, carry):
            m8s, l8s = carry
            base = pl.multiple_of(g * gk, gk)
            new_m8, new_l8 = [], []
            for hh in range(nh):
                cols = slice(hh * HEAD_W, (hh + 1) * HEAD_W)
                s = lax.dot_general(k_ref[pl.ds(base, gk), cols], qbd[hh],
                                    (((1,), (1,)), ((), ())), preferred_element_type=F32)
                sw_scr[hh, pl.ds(base, gk), :] = s
                new_m8.append(jnp.maximum(
                    m8s[hh], jnp.max(s.reshape(gk // SUBLANES, SUBLANES, 2 * tq), axis=0)))
                e = jnp.exp2(sr_scr[hh, pl.ds(base, gk), :] - m_prev[hh])
                new_l8.append(
                    l8s[hh] + jnp.sum(e.reshape(gk // SUBLANES, SUBLANES, 2 * tq), axis=0))
                acc_scr[hh] += jnp.dot(vt_ref[g, cols, :], e.astype(BF16),
                                       preferred_element_type=F32)
            return tuple(new_m8), tuple(new_l8)

        init = (tuple(jnp.full((SUBLANES, 2 * tq), -jnp.inf, F32) for _ in range(nh)),
                tuple(jnp.zeros((SUBLANES, 2 * tq), F32) for _ in range(nh)))
        m8s, l8s = init
        for g in range(ng):
            m8s, l8s = body(g, (m8s, l8s))
        for hh in range(nh):
            mw_scr[hh] = jnp.max(m8s[hh], axis=0, keepdims=True)
            _attn_finish(o_ref, sg_ref, hh, l8s[hh], acc_scr[hh], lam, lam_init)

    @pl.when(lax.rem(f, 2) == 0)
    def _():
        step(s0_scr, m0_scr, s1_scr, m1_scr)

    @pl.when(lax.rem(f, 2) == 1)
    def _():
        step(s1_scr, m1_scr, s0_scr, m0_scr)


def _attention_lat(qk, vt, lamv, sg, *, n_lat, tq, nh, lam_init):
    bsz, t, d2 = qk.shape
    d = d2 // 2
    hw = nh * HEAD_W
    heads = d // hw
    ng, gk = vt.shape[1], vt.shape[3]
    n_tiles = n_lat // tq
    last = heads * n_tiles - 1
    cur = lambda f: jnp.minimum(f, last)
    prev = lambda f: jnp.maximum(f - 1, 0)
    kern = functools.partial(_attn_pipe_kernel, lam_init=lam_init)
    return pl.pallas_call(
        kern,
        grid=(bsz, heads * n_tiles + 1),
        in_specs=[pl.BlockSpec((4, HEAD_W // 2), lambda b, f: (0, 0)),
                  pl.BlockSpec((HEAD_W, 1), lambda b, f: (0, 0)),
                  pl.BlockSpec((None, tq, hw), lambda b, f: (b, cur(f) % n_tiles, cur(f) // n_tiles)),
                  pl.BlockSpec((None, t, hw), lambda b, f: (b, 0, heads + cur(f) // n_tiles)),
                  pl.BlockSpec((None, ng, hw, gk), lambda b, f: (b, 0, prev(f) // n_tiles, 0))],
        out_specs=pl.BlockSpec((None, tq, hw),
                               lambda b, f: (b, prev(f) % n_tiles, prev(f) // n_tiles)),
        out_shape=jax.ShapeDtypeStruct((bsz, n_lat, d), BF16),
        scratch_shapes=[pltpu.VMEM((nh, t, 2 * tq), F32), pltpu.VMEM((nh, t, 2 * tq), F32),
                        pltpu.VMEM((nh, 1, 2 * tq), F32), pltpu.VMEM((nh, 1, 2 * tq), F32),
                        pltpu.VMEM((nh, HEAD_W, 2 * tq), F32)],
        compiler_params=_cparams(("arbitrary", "arbitrary")),
        name="diff_attn_lat",
    )(lamv, sg, qk, qk, vt)


def _attn_kernel(lam_ref, sg_ref, q_ref, k_ref, vt_ref, o_ref, s_scr, *, kc, v_off, lam_init):
    tq = q_ref.shape[0]
    tk = k_ref.shape[0]
    nh = q_ref.shape[1] // HEAD_W
    lam = _diff_lambda(lam_ref, lam_init)

    chunks = range(tk // kc)
    ms = []
    for hh in range(nh):
        qb = _split_q(q_ref, hh)
        m8 = jnp.full((SUBLANES, 2 * tq), -jnp.inf, F32)
        for c in chunks:
            s = lax.dot_general(k_ref[c * kc:(c + 1) * kc, hh * HEAD_W:(hh + 1) * HEAD_W], qb,
                                (((1,), (1,)), ((), ())), preferred_element_type=F32)
            s_scr[hh, c * kc:(c + 1) * kc, :] = s
            m8 = jnp.maximum(m8, jnp.max(s.reshape(kc // SUBLANES, SUBLANES, 2 * tq), axis=0))
        ms.append(jnp.max(m8, axis=0, keepdims=True))
    for hh in range(nh):
        l8 = jnp.zeros((SUBLANES, 2 * tq), F32)
        acc = jnp.zeros((HEAD_W, 2 * tq), F32)
        for c in chunks:
            e = jnp.exp2(s_scr[hh, c * kc:(c + 1) * kc, :] - ms[hh])
            l8 = l8 + jnp.sum(e.reshape(kc // SUBLANES, SUBLANES, 2 * tq), axis=0)
            acc = acc + jnp.dot(
                vt_ref[hh * HEAD_W:(hh + 1) * HEAD_W, v_off + c * kc:v_off + (c + 1) * kc],
                e.astype(BF16), preferred_element_type=F32)
        _attn_finish(o_ref, sg_ref, hh, l8, acc, lam, lam_init)


def _attention_ctx(qk, vt, lamv, sg, *, n_lat, nh, lam_init):
    bsz, t, d2 = qk.shape
    d = d2 // 2
    hw = nh * HEAD_W
    heads = d // hw
    n_ctx = t - n_lat
    gk = vt.shape[3]
    assert n_lat % n_ctx == 0 and n_lat % gk + n_ctx <= gk
    rb = n_lat // n_ctx
    kern = functools.partial(_attn_kernel, kc=n_ctx, v_off=n_lat % gk, lam_init=lam_init)
    return pl.pallas_call(
        kern,
        grid=(bsz, heads),
        in_specs=[pl.BlockSpec((4, HEAD_W // 2), lambda b, h: (0, 0)),
                  pl.BlockSpec((HEAD_W, 1), lambda b, h: (0, 0)),
                  pl.BlockSpec((None, n_ctx, hw), lambda b, h: (b, rb, h)),
                  pl.BlockSpec((None, n_ctx, hw), lambda b, h: (b, rb, heads + h)),
                  pl.BlockSpec((None, None, hw, gk), lambda b, h: (b, n_lat // gk, h, 0))],
        out_specs=pl.BlockSpec((None, n_ctx, hw), lambda b, h: (b, 0, h)),
        out_shape=jax.ShapeDtypeStruct((bsz, n_ctx, d), BF16),
        scratch_shapes=[pltpu.VMEM((nh, n_ctx, 2 * n_ctx), F32)],
        compiler_params=_cparams(("arbitrary", "arbitrary")),
        name="diff_attn_ctx",
    )(lamv, sg, qk, qk, vt)


def _router_probs(m, wr, n_exp):
    m_hi = m.astype(BF16)
    m_lo = (m - m_hi.astype(F32)).astype(BF16)
    w_hi = wr.astype(BF16)
    w_lo = (wr - w_hi.astype(F32)).astype(BF16)
    logits = (jnp.dot(m_hi, w_hi, preferred_element_type=F32)
              + jnp.dot(m_lo, w_hi, preferred_element_type=F32)
              + jnp.dot(m_hi, w_lo, preferred_element_type=F32))
    lane = lax.broadcasted_iota(I32, logits.shape, 1)
    lg = jnp.where(lane < n_exp, logits, -jnp.inf)
    ex = jnp.exp(lg - jnp.max(lg, axis=1, keepdims=True))
    return m_hi, ex / jnp.sum(ex, axis=1, keepdims=True)


def _proj_res_kernel(u_ref, w_ref, h_ref, gate_ref, o_ref, *, n_lat, ctx_row, nc):
    b = pl.program_id(0)
    i = pl.program_id(1)
    tt, d = h_ref.shape
    gate = _row_mod(gate_ref, b, ctx_row, i * tt, tt, n_lat)
    u = u_ref[...]
    for n in range(d // nc):
        y = jnp.dot(u, w_ref[:, n * nc:(n + 1) * nc], preferred_element_type=F32)
        o_ref[:, n * nc:(n + 1) * nc] = (h_ref[:, n * nc:(n + 1) * nc]
                                         + gate[:, n * nc:(n + 1) * nc] * y)


def _proj_res(u, w, h, mod, *, gate_chunk, n_rows, n_lat, tt):
    bsz, _, k = u.shape
    d = h.shape[-1]
    kern = functools.partial(_proj_res_kernel, n_lat=n_lat, ctx_row=bsz, nc=256)
    return pl.pallas_call(
        kern,
        grid=(bsz, n_rows // tt),
        in_specs=[pl.BlockSpec((None, tt, k), lambda b, i: (b, i, 0)),
                  pl.BlockSpec((k, d), lambda b, i: (0, 0)),
                  pl.BlockSpec((None, tt, d), lambda b, i: (b, i, 0)),
                  pl.BlockSpec((MOD_ROWS, d), lambda b, i: (0, gate_chunk))],
        out_specs=pl.BlockSpec((None, tt, d), lambda b, i: (b, i, 0)),
        out_shape=jax.ShapeDtypeStruct((bsz, n_rows, d), F32),
        compiler_params=_cparams(("arbitrary", "arbitrary")),
        name="proj_residual",
    )(u, w, h, mod)


def _norm_probs_kernel(h_ref, g_ref, sh_ref, sc_ref, wr_ref, m_ref, p_ref, *,
                       n_lat, ctx_row, n_exp):
    b = pl.program_id(0)
    i = pl.program_id(1)
    tt, d = h_ref.shape
    shift = _row_mod(sh_ref, b, ctx_row, i * tt, tt, n_lat)
    scale = _row_mod(sc_ref, b, ctx_row, i * tt, tt, n_lat)
    m = _ln_mod(h_ref[...], g_ref[...], shift, scale)
    m_ref[...], p_ref[...] = _router_probs(m, wr_ref[...], n_exp)


def _norm_probs(h, g, mod, wr, *, n_lat, n_exp, tt):
    bsz, t, d = h.shape
    kern = functools.partial(_norm_probs_kernel, n_lat=n_lat, ctx_row=bsz, n_exp=n_exp)
    return pl.pallas_call(
        kern,
        grid=(bsz, t // tt),
        in_specs=[pl.BlockSpec((None, tt, d), lambda b, i: (b, i, 0)),
                  pl.BlockSpec((1, d), lambda b, i: (0, 0)),
                  pl.BlockSpec((MOD_ROWS, d), lambda b, i: (0, 3)),
                  pl.BlockSpec((MOD_ROWS, d), lambda b, i: (0, 4)),
                  pl.BlockSpec((d, LANES), lambda b, i: (0, 0))],
        out_specs=[pl.BlockSpec((None, tt, d), lambda b, i: (b, i, 0)),
                   pl.BlockSpec((None, tt, LANES), lambda b, i: (b, i, 0))],
        out_shape=[jax.ShapeDtypeStruct((bsz, t, d), BF16),
                   jax.ShapeDtypeStruct((bsz, t, LANES), F32)],
        compiler_params=_cparams(("arbitrary", "arbitrary")),
        name="moe_norm_probs",
    )(h, g, mod, mod, wr)


def _lru_in_kernel(h_ref, g_ref, sh_ref, sc_ref, w_ref, bias_ref, gy_ref, xr_ref, *,
                   n_lat, ctx_row, nc):
    b = pl.program_id(0)
    i = pl.program_id(1)
    tt, d = h_ref.shape
    shift = _row_mod(sh_ref, b, ctx_row, i * tt, tt, n_lat)
    scale = _row_mod(sc_ref, b, ctx_row, i * tt, tt, n_lat)
    xn = _ln_mod(h_ref[...], g_ref[...], shift, scale).astype(BF16)
    for n in range(d // nc):
        y = jnp.dot(xn, w_ref[:, n * nc:(n + 1) * nc], preferred_element_type=F32)
        y = y + bias_ref[:, n * nc:(n + 1) * nc]
        gy_ref[:, n * nc:(n + 1) * nc] = _gelu_tanh(y).astype(BF16)
    for n in range(d // nc):
        x = jnp.dot(xn, w_ref[:, d + n * nc:d + (n + 1) * nc], preferred_element_type=F32)
        xr_ref[:, n * nc:(n + 1) * nc] = x + bias_ref[:, d + n * nc:d + (n + 1) * nc]


def _lru_in(h, g, mod, w_in, b_in, *, n_lat, tt):
    bsz, t, d = h.shape
    kern = functools.partial(_lru_in_kernel, n_lat=n_lat, ctx_row=bsz, nc=256)
    return pl.pallas_call(
        kern,
        grid=(bsz, t // tt),
        in_specs=[pl.BlockSpec((None, tt, d), lambda b, i: (b, i, 0)),
                  pl.BlockSpec((1, d), lambda b, i: (0, 0)),
                  pl.BlockSpec((MOD_ROWS, d), lambda b, i: (0, 0)),
                  pl.BlockSpec((MOD_ROWS, d), lambda b, i: (0, 1)),
                  pl.BlockSpec((d, 2 * d), lambda b, i: (0, 0)),
                  pl.BlockSpec((1, 2 * d), lambda b, i: (0, 0))],
        out_specs=[pl.BlockSpec((None, tt, d), lambda b, i: (b, i, 0)),
                   pl.BlockSpec((None, tt, d), lambda b, i: (b, i, 0))],
        out_shape=[jax.ShapeDtypeStruct((bsz, t, d), BF16),
                   jax.ShapeDtypeStruct((bsz, t, d), F32)],
        compiler_params=_cparams(("arbitrary", "arbitrary")),
        name="lru_in_proj",
    )(h, g, mod, mod, w_in, b_in)


def _lru_kernel(xr_ref, gy_ref, cw_ref, cb_ref, wg_ref, bg_ref, lam_ref, u_ref,
                xp_scr, hf_scr, hb_scr, *, n_lat):
    t, cw = xr_ref.shape
    n_ctx = t - n_lat
    seg, pitch, nsub = LRU_SEG, LRU_PITCH, SUBLANES
    rc = seg * nsub
    nseg_lat, nseg_ctx = n_lat // seg, n_ctx // seg
    lat0, ctx0 = 1, nseg_lat + 2
    zseg = jnp.zeros((seg, cw), F32)
    for s in (0, nseg_lat + 1, nseg_lat + nseg_ctx + 2):
        xp_scr[s * pitch:s * pitch + seg, :] = zseg
    for s in range(nseg_lat):
        xp_scr[(lat0 + s) * pitch:(lat0 + s) * pitch + seg, :] = xr_ref[s * seg:(s + 1) * seg, :]
    for s in range(nseg_ctx):
        xp_scr[(ctx0 + s) * pitch:(ctx0 + s) * pitch + seg, :] = (
            xr_ref[n_lat + s * seg:n_lat + (s + 1) * seg, :])
    lam = lam_ref[...]
    sp = jnp.maximum(-lam, 0.0) + jnp.log(1.0 + jnp.exp(-jnp.abs(lam)))
    w = cw_ref[...]
    cb = cb_ref[...]

    def rows8(ref, start):
        return ref[pl.ds(start, nsub, stride=pitch), :]

    def chunk(pb, carry, d, reverse, out_scr):
        x = [rows8(xp_scr, pb + g) for g in range(seg)]
        xm1 = [rows8(xp_scr, pb - (pitch - seg) - 1)] + x[:-1]
        xm2 = [rows8(xp_scr, pb - (pitch - seg) - 2), xm1[0]] + x[:-2]
        xp1 = x[1:] + [rows8(xp_scr, pb + pitch)]
        xc = jnp.concatenate(
            [cb + w[0:1, :] * xm2[g] + w[1:2, :] * xm1[g] + w[2:3, :] * x[g] + w[3:4, :] * xp1[g]
             for g in range(seg)], axis=0)
        gpre = jnp.dot(xc.astype(BF16), wg_ref[d, 0], preferred_element_type=F32) + bg_ref[d, 0]
        r = _sigmoid(gpre[:, :LRU_BW])
        ig = _sigmoid(gpre[:, LRU_BW:])
        a = jnp.exp((-LRU_C * sp[d:d + 1, :]) * r)
        bt = jnp.sqrt(1.0 - a * a) * (ig * xc)
        a3 = a.reshape(seg, nsub, cw)
        b3 = bt.reshape(seg, nsub, cw)
        h = jnp.zeros((nsub, cw), F32)
        p = jnp.ones((nsub, cw), F32)
        hl, pl_ = [None] * seg, [None] * seg
        for g in (range(seg - 1, -1, -1) if reverse else range(seg)):
            h = a3[g] * h + b3[g]
            p = a3[g] * p
            hl[g], pl_[g] = h, p
        entry = [None] * nsub
        for j in (range(nsub - 1, -1, -1) if reverse else range(nsub)):
            entry[j] = carry
            carry = h[j:j + 1, :] + p[j:j + 1, :] * carry
        h_in = jnp.concatenate(entry, axis=0)
        for g in range(seg):
            out_scr[pl.ds(pb + g, nsub, stride=pitch), :] = hl[g] + pl_[g] * h_in
        return carry

    lat_pb = lambda c: lat0 * pitch + c * (nsub * pitch)
    ctx_pb = lambda c: (ctx0 + c * nsub) * pitch
    n_lat_chunks, n_ctx_chunks = n_lat // rc, n_ctx // rc
    cf = cb_ = jnp.zeros((1, cw), F32)
    for c in range(n_ctx_chunks):
        cf = chunk(ctx_pb(c), cf, 0, False, hf_scr)
        cb_ = chunk(ctx_pb(n_ctx_chunks - 1 - c), cb_, 1, True, hb_scr)

    def both(c, carries):
        return (chunk(lat_pb(c), carries[0], 0, False, hf_scr),
                chunk(lat_pb(n_lat_chunks - 1 - c), carries[1], 1, True, hb_scr))

    lax.fori_loop(0, n_lat_chunks, both, (cf, cb_))
    for s in range(nseg_lat):
        rows = slice((lat0 + s) * pitch, (lat0 + s) * pitch + seg)
        u_ref[s * seg:(s + 1) * seg, :] = (gy_ref[s * seg:(s + 1) * seg, :].astype(F32)
                                           * (hf_scr[rows, :] + hb_scr[rows, :])).astype(BF16)


def _lru_core(xr, gy, conv_w, conv_b, w_gates, b_gates, lam, *, n_lat):
    bsz, t, d = xr.shape
    cw = LRU_BW
    rows = ((t // LRU_SEG) + 3) * LRU_PITCH
    kern = functools.partial(_lru_kernel, n_lat=n_lat)
    return pl.pallas_call(
        kern,
        grid=(bsz, d // cw),
        in_specs=[pl.BlockSpec((None, t, cw), lambda b, k: (b, 0, k)),
                  pl.BlockSpec((None, n_lat, cw), lambda b, k: (b, 0, k)),
                  pl.BlockSpec((CONV_W, cw), lambda b, k: (0, k)),
                  pl.BlockSpec((1, cw), lambda b, k: (0, k)),
                  pl.BlockSpec((2, 1, LRU_BW, 2 * LRU_BW), lambda b, k: (0, k, 0, 0)),
                  pl.BlockSpec((2, 1, 1, 2 * LRU_BW), lambda b, k: (0, k, 0, 0)),
                  pl.BlockSpec((2, cw), lambda b, k: (0, k))],
        out_specs=pl.BlockSpec((None, n_lat, cw), lambda b, k: (b, 0, k)),
        out_shape=jax.ShapeDtypeStruct((bsz, n_lat, d), BF16),
        scratch_shapes=[pltpu.VMEM((rows, cw), F32)] * 3,
        compiler_params=_cparams(("arbitrary", "arbitrary")),
        name="lru_core",
    )(xr, gy, conv_w, conv_b, w_gates, b_gates, lam)


def _select(p, cap, n_exp):
    n_tok = p.shape[0]
    n_rows = -(-n_exp // SUBLANES) * SUBLANES
    pt = p.T[0:n_rows, :]

    def body(i, thr):
        cand = thr | jnp.left_shift(jnp.int32(1), 29 - i)
        hit = jnp.where(pt >= lax.bitcast_convert_type(cand, F32), 1.0, 0.0)
        return jnp.where(jnp.sum(hit, axis=1, keepdims=True) >= cap, cand, thr)

    thr_col = lax.fori_loop(0, 30, body, jnp.zeros((n_rows, 1), I32))
    diag = (lax.broadcasted_iota(I32, (n_rows, LANES), 0)
            == lax.broadcasted_iota(I32, (n_rows, LANES), 1))
    thr = jnp.sum(jnp.where(diag, jnp.broadcast_to(thr_col, (n_rows, LANES)), 0),
                  axis=0, keepdims=True)
    gt = p >= pltpu.bitcast(thr + 1, F32)
    eq = (p >= pltpu.bitcast(thr, F32)) & jnp.logical_not(gt)
    blk = PREFIX_BLOCK
    ltri = jnp.where(lax.broadcasted_iota(I32, (blk, blk), 1)
                     < lax.broadcasted_iota(I32, (blk, blk), 0), 1.0, 0.0).astype(BF16)
    masks = jnp.concatenate([jnp.where(gt, 1.0, 0.0), jnp.where(eq, 1.0, 0.0)], axis=1)
    off = jnp.zeros((1, 2 * LANES), F32)
    pres = []
    for i in range(n_tok // blk):
        mb = masks[i * blk:(i + 1) * blk, :]
        pre = jnp.dot(ltri, mb.astype(BF16), preferred_element_type=F32) + off
        pres.append(pre)
        off = pre[blk - 1:blk, :] + mb[blk - 1:blk, :]
    pre = jnp.concatenate(pres, axis=0)
    pre_gt, pre_eq = pre[:, :LANES], pre[:, LANES:]
    need = cap - off[:, :LANES]
    sel = gt | (eq & (pre_eq < need))
    slot = pre_gt + jnp.minimum(pre_eq, need)
    bounds = jnp.concatenate([slot[i * blk:i * blk + 1, :] for i in range(n_tok // blk)]
                             + [jnp.full((1, LANES), float(cap), F32)], axis=0)
    return jnp.where(sel, slot, -1.0), jnp.where(sel, p, 0.0), bounds


def _select_kernel(p_ref, slot_ref, gate_ref, slot_t_ref, bounds_ref, *, n_lat, n_exp):
    t = p_ref.shape[0]
    n_ctx = t - n_lat
    slot, gate, bounds = _select(p_ref[0:n_lat, :], EC_CAPACITY * n_lat // n_exp, n_exp)
    slot_ref[0:n_lat, :] = slot
    gate_ref[0:n_lat, :] = gate
    slot_t_ref[:, 0:n_lat] = slot.T
    bounds_ref[...] = jnp.zeros_like(bounds_ref)
    bounds_ref[0:bounds.shape[0], :] = bounds
    if n_ctx:
        slot, gate, _ = _select(p_ref[n_lat:t, :], EC_CAPACITY * n_ctx // n_exp, n_exp)
        slot_ref[n_lat:t, :] = slot
        gate_ref[n_lat:t, :] = gate
        slot_t_ref[:, n_lat:t] = slot.T


def _select_tokens(p, *, n_lat, n_exp):
    bsz, t, _ = p.shape
    nb_rows = -(-(n_lat // PREFIX_BLOCK + 1) // SUBLANES) * SUBLANES
    kern = functools.partial(_select_kernel, n_lat=n_lat, n_exp=n_exp)
    return pl.pallas_call(
        kern,
        grid=(bsz,),
        in_specs=[pl.BlockSpec((None, t, LANES), lambda b: (b, 0, 0))],
        out_specs=[pl.BlockSpec((None, t, LANES), lambda b: (b, 0, 0)),
                   pl.BlockSpec((None, t, LANES), lambda b: (b, 0, 0)),
                   pl.BlockSpec((None, LANES, t), lambda b: (b, 0, 0)),
                   pl.BlockSpec((None, nb_rows, LANES), lambda b: (b, 0, 0))],
        out_shape=[jax.ShapeDtypeStruct((bsz, t, LANES), F32),
                   jax.ShapeDtypeStruct((bsz, t, LANES), F32),
                   jax.ShapeDtypeStruct((bsz, LANES, t), F32),
                   jax.ShapeDtypeStruct((bsz, nb_rows, LANES), F32)],
        compiler_params=_cparams(("arbitrary",)),
        name="moe_select",
    )(p)


def _gather_kernel(bounds_ref, m_ref, slot_t_ref, xg_ref, acc_scr, *, n_lat, cap_lat, cap_ctx):
    b = pl.program_id(0)
    kb = pl.program_id(1)
    n_exp = xg_ref.shape[0]
    blk, d = m_ref.shape
    n_lat_blocks = n_lat // blk

    @pl.when(kb == 0)
    def _():
        acc_scr[...] = jnp.zeros_like(acc_scr)

    base = (b * (n_lat_blocks + 1) + jnp.minimum(kb, n_lat_blocks - 1)) * n_exp
    los = [bounds_ref[base + e] for e in range(n_exp)]
    his = [bounds_ref[base + n_exp + e] for e in range(n_exp)]
    taken = kb >= n_lat_blocks

    def one_hot(rows, first):
        sl = lax.broadcasted_iota(I32, (rows, 1), 0).astype(F32)
        return [jnp.where(slot_t_ref[e:e + 1, :] - first[e] == sl, 1.0, 0.0).astype(BF16)
                for e in range(n_exp)]

    def window_path(win, starts):
        p = jnp.concatenate(one_hot(win, [s.astype(F32) for s in starts]), axis=0)
        res = jnp.dot(p, m_ref[...], preferred_element_type=F32)
        for e in range(n_exp):
            acc_scr[e, pl.ds(starts[e], win), :] += res[e * win:(e + 1) * win, :]

    for win in (64, 128):
        if win > cap_lat:
            continue
        starts, fits = [], None
        for e in range(n_exp):
            st = jnp.minimum(lax.shift_left(lax.shift_right_logical(los[e], 4), 4),
                             cap_lat - win)
            starts.append(pl.multiple_of(st, 16))
            ok = his[e] - st <= win
            fits = ok if fits is None else jnp.logical_and(fits, ok)
        pl.when(jnp.logical_and(jnp.logical_not(taken), fits))(
            functools.partial(window_path, win, starts))
        taken = jnp.logical_or(taken, fits)

    @pl.when(jnp.logical_not(taken))
    def _():
        p = jnp.concatenate(one_hot(cap_lat, [0.0] * n_exp), axis=0)
        res = jnp.dot(p, m_ref[...], preferred_element_type=F32)
        for e in range(n_exp):
            acc_scr[e] += res[e * cap_lat:(e + 1) * cap_lat, :]

    @pl.when(kb == n_lat_blocks - 1)
    def _():
        for e in range(n_exp):
            xg_ref[e, 0:cap_lat, :] = acc_scr[e].astype(BF16)

    if cap_ctx:
        @pl.when(kb >= n_lat_blocks)
        def _():
            p = jnp.concatenate(one_hot(cap_ctx, [0.0] * n_exp), axis=0)
            res = jnp.dot(p, m_ref[...], preferred_element_type=F32).astype(BF16)
            for e in range(n_exp):
                xg_ref[e, cap_lat:cap_lat + cap_ctx, :] = res[e * cap_ctx:(e + 1) * cap_ctx, :]


def _gather(m, slot_t, table, *, n_lat, n_exp):
    bsz, t, d = m.shape
    blk = PREFIX_BLOCK
    cap_lat = EC_CAPACITY * n_lat // n_exp
    cap_ctx = EC_CAPACITY * (t - n_lat) // n_exp
    assert t - n_lat in (0, blk)
    r = cap_lat + cap_ctx
    kern = functools.partial(_gather_kernel, n_lat=n_lat, cap_lat=cap_lat, cap_ctx=cap_ctx)
    grid_spec = pltpu.PrefetchScalarGridSpec(
        num_scalar_prefetch=1,
        grid=(bsz, t // blk),
        in_specs=[pl.BlockSpec((None, blk, d), lambda b, kb, tbl: (b, kb, 0)),
                  pl.BlockSpec((None, LANES, blk), lambda b, kb, tbl: (b, 0, kb))],
        out_specs=pl.BlockSpec((n_exp, None, r, d), lambda b, kb, tbl: (0, b, 0, 0)),
        scratch_shapes=[pltpu.VMEM((n_exp, cap_lat, d), F32)])
    return pl.pallas_call(
        kern,
        grid_spec=grid_spec,
        out_shape=jax.ShapeDtypeStruct((n_exp, bsz, r, d), BF16),
        compiler_params=_cparams(("arbitrary", "arbitrary")),
        name="moe_gather",
    )(table, m, slot_t)


def _ffn_kernel(x_ref, wg_ref, wu_ref, wd_ref, y_ref, acc_scr, *, rc):
    j = pl.program_id(1)
    nj = pl.num_programs(1)
    rows = x_ref.shape[0]

    @pl.when(j == 0)
    def _():
        acc_scr[...] = jnp.zeros_like(acc_scr)

    wg = wg_ref[...].astype(BF16)
    wu = wu_ref[...].astype(BF16)
    wd = wd_ref[...].astype(BF16)

    for c in range(rows // rc):
        x = x_ref[c * rc:(c + 1) * rc, :]
        g = jnp.dot(x, wg, preferred_element_type=F32)
        u = jnp.dot(x, wu, preferred_element_type=F32)
        hid = ((g * _sigmoid(g)) * u).astype(BF16)
        acc_scr[c * rc:(c + 1) * rc, :] += jnp.dot(hid, wd, preferred_element_type=F32)

    @pl.when(j == nj - 1)
    def _():
        y_ref[...] = acc_scr[...].astype(BF16)


def _ffn(xg, w_gate_up, w_down, *, layer, fc, rc):
    n_exp, rows, d = xg.shape
    f = w_down.shape[2]
    nj = f // fc
    kern = functools.partial(_ffn_kernel, rc=rc)
    return pl.pallas_call(
        kern,
        grid=(n_exp, nj),
        in_specs=[pl.BlockSpec((None, rows, d), lambda e, j: (e, 0, 0)),
                  pl.BlockSpec((None, None, d, fc), lambda e, j: (layer, e, 0, j)),
                  pl.BlockSpec((None, None, d, fc), lambda e, j: (layer, e, 0, nj + j)),
                  pl.BlockSpec((None, None, fc, d), lambda e, j: (layer, e, j, 0))],
        out_specs=pl.BlockSpec((None, rows, d), lambda e, j: (e, 0, 0)),
        out_shape=jax.ShapeDtypeStruct((n_exp, rows, d), BF16),
        scratch_shapes=[pltpu.VMEM((rows, d), F32)],
        compiler_params=_cparams(("arbitrary", "arbitrary")),
        name="moe_ffn",
    )(xg, w_gate_up, w_gate_up, w_down)


def _combine_kernel(bounds_ref, y_ref, slot_ref, gate_ref, h_ref, g2_ref, fg_ref, o_ref, *,
                    n_lat, ctx_row, cap_lat, cap_ctx, final_norm):
    b = pl.program_id(0)
    i = pl.program_id(1)
    tt, d = h_ref.shape
    n_exp = y_ref.shape[0]
    n_lat_tiles = n_lat // tt

    def finish(acc):
        gate2 = _row_mod(g2_ref, b, ctx_row, i * tt, tt, n_lat)
        out = h_ref[...] + gate2 * acc
        if final_norm:
            ms = jnp.mean(out * out, axis=-1, keepdims=True)
            out = (out * lax.rsqrt(ms + EPS)) * fg_ref[...]
        o_ref[...] = out

    win = min(cap_lat, LANES)
    base = (b * (n_lat_tiles + 1) + jnp.minimum(i, n_lat_tiles - 1)) * n_exp
    starts = []
    fits = None
    for e in range(n_exp):
        lo = bounds_ref[base + e]
        hi = bounds_ref[base + n_exp + e]
        st = jnp.minimum(lax.shift_left(lax.shift_right_logical(lo, 4), 4), cap_lat - win)
        starts.append(pl.multiple_of(st, 16))
        ok = hi - st <= win
        fits = ok if fits is None else jnp.logical_and(fits, ok)
    is_lat = i < n_lat_tiles

    @pl.when(jnp.logical_and(is_lat, fits))
    def _():
        slot = slot_ref[...]
        gate = gate_ref[...]
        iota_w = lax.broadcasted_iota(I32, (1, win), 1).astype(F32)
        acc = jnp.zeros((tt, d), F32)
        for e0 in range(0, n_exp, 2):
            pts, ys = [], []
            for e in range(e0, min(e0 + 2, n_exp)):
                rel = slot[:, e:e + 1] - starts[e].astype(F32)
                pts.append(jnp.where(rel == iota_w, gate[:, e:e + 1], 0.0))
                ys.append(y_ref[e, pl.ds(starts[e], win), :])
            acc = acc + jnp.dot(jnp.concatenate(pts, axis=1).astype(BF16),
                                jnp.concatenate(ys, axis=0), preferred_element_type=F32)
        finish(acc)

    @pl.when(jnp.logical_and(is_lat, jnp.logical_not(fits)))
    def _():
        slot = slot_ref[...]
        gate = gate_ref[...]
        iota_c = lax.broadcasted_iota(I32, (1, cap_lat), 1).astype(F32)
        acc = jnp.zeros((tt, d), F32)
        for e in range(n_exp):
            pt = jnp.where(slot[:, e:e + 1] == iota_c, gate[:, e:e + 1], 0.0).astype(BF16)
            acc = acc + jnp.dot(pt, y_ref[e, 0:cap_lat, :], preferred_element_type=F32)
        finish(acc)

    if cap_ctx:
        @pl.when(i >= n_lat_tiles)
        def _():
            slot = slot_ref[...]
            gate = gate_ref[...]
            iota_c = lax.broadcasted_iota(I32, (1, n_exp * cap_ctx), 1).astype(F32)
            pt = jnp.zeros((tt, n_exp * cap_ctx), F32)
            for e in range(n_exp):
                pt = pt + jnp.where(slot[:, e:e + 1] + float(e * cap_ctx) == iota_c,
                                    gate[:, e:e + 1], 0.0)
            yc = y_ref[:, cap_lat:cap_lat + cap_ctx, :].reshape(n_exp * cap_ctx, d)
            finish(jnp.dot(pt.astype(BF16), yc, preferred_element_type=F32))


def _combine(y, slot, gate, table, h, mod, final_g, *, n_lat, n_out, final_norm):
    n_exp, bsz, r, d = y.shape
    t = slot.shape[1]
    tt = PREFIX_BLOCK
    cap_lat = EC_CAPACITY * n_lat // n_exp
    cap_ctx = EC_CAPACITY * (t - n_lat) // n_exp
    kern = functools.partial(_combine_kernel, n_lat=n_lat, ctx_row=bsz, cap_lat=cap_lat,
                             cap_ctx=cap_ctx, final_norm=final_norm)
    grid_spec = pltpu.PrefetchScalarGridSpec(
        num_scalar_prefetch=1,
        grid=(bsz, n_out // tt),
        in_specs=[pl.BlockSpec((n_exp, None, r, d), lambda b, i, tbl: (0, b, 0, 0)),
                  pl.BlockSpec((None, tt, LANES), lambda b, i, tbl: (b, i, 0)),
                  pl.BlockSpec((None, tt, LANES), lambda b, i, tbl: (b, i, 0)),
                  pl.BlockSpec((None, tt, d), lambda b, i, tbl: (b, i, 0)),
                  pl.BlockSpec((MOD_ROWS, d), lambda b, i, tbl: (0, 5)),
                  pl.BlockSpec((1, d), lambda b, i, tbl: (0, 0))],
        out_specs=pl.BlockSpec((None, tt, d), lambda b, i, tbl: (b, i, 0)))
    return pl.pallas_call(
        kern,
        grid_spec=grid_spec,
        out_shape=jax.ShapeDtypeStruct((bsz, n_out, d), F32),
        compiler_params=_cparams(("arbitrary", "arbitrary")),
        name="moe_combine",
    )(table, y, slot, gate, h, mod, final_g)


def _moe(h, m, p, mod, w_gate_up, w_down, final_g, *, layer, n_lat, n_exp, final_norm):
    bsz, t, d = h.shape
    slot, gate, slot_t, bounds = _select_tokens(p, n_lat=n_lat, n_exp=n_exp)
    table = bounds[:, :n_lat // PREFIX_BLOCK + 1, :n_exp].astype(I32).reshape(-1)
    xg = _gather(m, slot_t, table, n_lat=n_lat, n_exp=n_exp)
    r = xg.shape[2]
    y = _ffn(xg.reshape(n_exp, bsz * r, d), w_gate_up, w_down, layer=layer, fc=256,
             rc=_pick(bsz * r, (768, 1024, 512, 256, 128, 64, 32, 16)))
    return _combine(y.reshape(n_exp, bsz, r, d), slot, gate, table, h, mod, final_g,
                    n_lat=n_lat, n_out=t, final_norm=final_norm)


def kernel(x, c, ctx, c_ctx, ada_w, ada_b, norm1_g, norm2_g, final_g, attn_w_qkv, attn_lq1, attn_lk1, attn_lq2, attn_lk2, attn_subln_g, attn_w_o, lru_w_in, lru_b_in, lru_conv_w, lru_conv_b, lru_w_gates, lru_b_gates, lru_lambda, lru_w_out, moe_w_router, moe_w_gate_up, moe_w_down):
    bsz, n_lat, d = x.shape
    n_ctx = ctx.shape[1]
    depth = ada_w.shape[0]
    assert bsz < MOD_ROWS and d % (2 * HEAD_W) == 0 and n_lat % GRID_W == 0

    cc = jnp.concatenate([c, c_ctx[None, :], jnp.zeros((MOD_ROWS - bsz - 1, d), F32)], axis=0)
    mods = _mod_tables(cc, ada_w, ada_b)
    h = jnp.concatenate([x, ctx], axis=1)
    row = lambda v: v.reshape(1, -1)

    for i in range(depth):
        last = i == depth - 1
        mod = mods[i]
        j = i // N_MIXERS
        n_exp = moe_w_router.shape[-1]
        wr = jnp.pad(moe_w_router[i], ((0, 0), (0, LANES - n_exp)))
        if i % N_MIXERS == 0:
            lam_init = 0.8 - 0.6 * math.exp(-0.3 * i)
            w = attn_w_qkv[j]
            wqk = w[:, :2 * d].astype(BF16)
            wvt = w[:, 2 * d:].T.astype(BF16)
            cos, s1, s2 = _rope_tables(n_lat, n_ctx)
            t_all = n_lat + n_ctx
            qk, vt = _qkv(h, row(norm1_g[i]), mod, wqk, wvt, cos, s1, s2, n_lat=n_lat,
                          tt=_pick(t_all, (768, 256)), vs=1)
            lamv = jnp.stack([attn_lq1[j], attn_lk1[j], attn_lq2[j], attn_lk2[j]], axis=0)
            sg = attn_subln_g[j].reshape(HEAD_W, 1)
            n_rows = n_lat if last else n_lat + n_ctx
            o = _attention_lat(qk, vt, lamv, sg, n_lat=n_lat, tq=256, nh=2, lam_init=lam_init)
            if not last:
                o_ctx = _attention_ctx(qk, vt, lamv, sg, n_lat=n_lat, nh=2, lam_init=lam_init)
                o = jnp.concatenate([o, o_ctx], axis=1)
            h = _proj_res(o, attn_w_o[j].astype(BF16), h, mod, gate_chunk=2, n_rows=n_rows,
                          n_lat=n_lat, tt=_pick(n_rows, (768, 512, 256)))
        else:
            gy, xr = _lru_in(h, row(norm1_g[i]), mod, lru_w_in[j].astype(BF16),
                             row(lru_b_in[j]), n_lat=n_lat, tt=_pick(n_lat + n_ctx, (768, 256)))
            nblk = d // LRU_BW
            u = _lru_core(xr, gy, lru_conv_w[j], row(lru_conv_b[j]),
                          lru_w_gates[j].astype(BF16),
                          lru_b_gates[j].reshape(2, nblk, 1, 2 * LRU_BW), lru_lambda[j],
                          n_lat=n_lat)
            n_rows = n_lat if last else n_lat + n_ctx
            assert last, "context output of the recurrent mixer is only needed in non-final layers"
            h = _proj_res(u, lru_w_out[j].astype(BF16), h, mod, gate_chunk=2, n_rows=n_rows,
                          n_lat=n_lat, tt=_pick(n_rows, (1024, 512, 256)))
        m, p = _norm_probs(h, row(norm2_g[i]), mod, wr, n_lat=n_lat, n_exp=n_exp,
                           tt=_pick(h.shape[1], (768, 512, 256)))
        h = _moe(h, m, p, mod, moe_w_gate_up, moe_w_down, row(final_g), layer=i, n_lat=n_lat,
                 n_exp=n_exp, final_norm=last)
    return h
```

```python
import functools
import math

import jax
import jax.numpy as jnp
from jax import lax
from jax.experimental import pallas as pl
from jax.experimental.pallas import tpu as pltpu

F32 = jnp.float32
BF16 = jnp.bfloat16
I32 = jnp.int32

EPS = 1e-6
GRID_W = 64
ROPE_THETA = 10000.0
LRU_C = 8.0
EC_CAPACITY = 2
N_MIXERS = 2
HEAD_W = 128
LRU_BW = 128
LRU_SEG = 32
LRU_PITCH = 40
CONV_W = 4
MOD_ROWS = 16
PREFIX_BLOCK = 256
LANES = 128
SUBLANES = 8
VMEM_LIMIT = 56 * 2**20


def _pick(n, cands):
    return next(c for c in cands if n % c == 0)


def _cparams(sem, flags=None):
    return pltpu.CompilerParams(dimension_semantics=sem, vmem_limit_bytes=VMEM_LIMIT, flags=flags)


def _sigmoid(x):
    return 0.5 * jnp.tanh(0.5 * x) + 0.5


def _gelu_tanh(x):
    return 0.5 * x * (1.0 + jnp.tanh(math.sqrt(2.0 / math.pi) * (x + 0.044715 * (x * x * x))))


def _row_mod(mod_ref, b, ctx_row, row0, nrows, n_lat):
    mb = mod_ref[pl.ds(b, 1), :]
    mc = mod_ref[ctx_row:ctx_row + 1, :]
    rows = row0 + lax.broadcasted_iota(I32, (nrows, 1), 0)
    return jnp.where(rows < n_lat, mb, mc)


def _ln_mod(x, g, shift, scale):
    ms = jnp.mean(x * x, axis=-1, keepdims=True)
    y = (x * lax.rsqrt(ms + EPS)) * g
    return y * (1.0 + scale) + shift


def _mod_kernel(c_ref, w_ref, b_ref, o_ref):
    c = c_ref[...]
    s = c * _sigmoid(c)
    o_ref[...] = jnp.dot(s, w_ref[...], preferred_element_type=F32,
                         precision=lax.Precision.HIGHEST) + b_ref[...]


def _mod_tables(cc, ada_w, ada_b):
    depth, d, n6 = ada_w.shape
    tn = _pick(n6, (1536, 768, 384))
    return pl.pallas_call(
        _mod_kernel,
        grid=(depth, n6 // tn),
        in_specs=[pl.BlockSpec((MOD_ROWS, d), lambda i, j: (0, 0)),
                  pl.BlockSpec((None, d, tn), lambda i, j: (i, 0, j)),
                  pl.BlockSpec((None, 1, tn), lambda i, j: (i, 0, j))],
        out_specs=pl.BlockSpec((None, MOD_ROWS, tn), lambda i, j: (i, 0, j)),
        out_shape=jax.ShapeDtypeStruct((depth, MOD_ROWS, n6), F32),
        compiler_params=_cparams(("arbitrary", "arbitrary")),
        name="mod_tables",
    )(cc, ada_w, ada_b.reshape(depth, 1, n6))


def _qkv_kernel(h_ref, g_ref, sh_ref, sc_ref, wqk_ref, wvt_ref, cos_ref, s1_ref, s2_ref,
                qk_ref, vt_ref, *, n_lat, ctx_row):
    b = pl.program_id(0)
    i = pl.program_id(1)
    tt, d = h_ref.shape
    shift = _row_mod(sh_ref, b, ctx_row, i * tt, tt, n_lat)
    scale = _row_mod(sc_ref, b, ctx_row, i * tt, tt, n_lat)
    xn = _ln_mod(h_ref[...], g_ref[...], shift, scale).astype(BF16)
    cw = 2 * HEAD_W
    cos = jnp.concatenate([cos_ref[...]] * 2, axis=1)
    s1 = jnp.concatenate([s1_ref[...]] * 2, axis=1)
    s2 = jnp.concatenate([s2_ref[...]] * 2, axis=1)
    for n in range(2 * d // cw):
        acc = jnp.dot(xn, wqk_ref[:, n * cw:(n + 1) * cw], preferred_element_type=F32)
        r = acc * cos + pltpu.roll(acc, cw - 16, 1) * s1 + pltpu.roll(acc, 16, 1) * s2
        if n * cw < d:
            r = r * ((0.5 * HEAD_W) ** -0.5 * math.log2(math.e))
        qk_ref[:, n * cw:(n + 1) * cw] = r.astype(BF16)
    for n in range(d // cw):
        vt = lax.dot_general(wvt_ref[n * cw:(n + 1) * cw, :], xn, (((1,), (1,)), ((), ())),
                             preferred_element_type=F32).astype(BF16)
        gk = vt_ref.shape[2]
        for s in range(vt_ref.shape[0]):
            vt_ref[s, n * cw:(n + 1) * cw, :] = vt[:, s * gk:(s + 1) * gk]


def _qkv(h, g, mod, wqk, wvt, cos, s1, s2, *, n_lat, tt, vs):
    bsz, t, d = h.shape
    gk = tt // vs
    kern = functools.partial(_qkv_kernel, n_lat=n_lat, ctx_row=bsz)
    return pl.pallas_call(
        kern,
        grid=(bsz, t // tt),
        in_specs=[pl.BlockSpec((None, tt, d), lambda b, i: (b, i, 0)),
                  pl.BlockSpec((1, d), lambda b, i: (0, 0)),
                  pl.BlockSpec((MOD_ROWS, d), lambda b, i: (0, 0)),
                  pl.BlockSpec((MOD_ROWS, d), lambda b, i: (0, 1)),
                  pl.BlockSpec((d, 2 * d), lambda b, i: (0, 0)),
                  pl.BlockSpec((d, d), lambda b, i: (0, 0)),
                  pl.BlockSpec((tt, HEAD_W), lambda b, i: (i, 0)),
                  pl.BlockSpec((tt, HEAD_W), lambda b, i: (i, 0)),
                  pl.BlockSpec((tt, HEAD_W), lambda b, i: (i, 0))],
        out_specs=[pl.BlockSpec((None, tt, 2 * d), lambda b, i: (b, i, 0)),
                   pl.BlockSpec((None, vs, d, gk), lambda b, i: (b, i, 0, 0))],
        out_shape=[jax.ShapeDtypeStruct((bsz, t, 2 * d), BF16),
                   jax.ShapeDtypeStruct((bsz, t // gk, d, gk), BF16)],
        compiler_params=_cparams(("arbitrary", "arbitrary")),
        name="qkv_rope",
    )(h, g, mod, mod, wqk, wvt, cos, s1, s2)


def _rope_tables(n_lat, n_ctx):
    freqs = HEAD_W // 8
    rows = n_lat // GRID_W
    row = jnp.repeat(jnp.arange(rows), GRID_W).astype(F32)
    col = jnp.tile(jnp.arange(GRID_W), rows).astype(F32)
    inv = ROPE_THETA ** (-(jnp.arange(freqs, dtype=F32) * 2.0) / (2 * freqs))
    lane = jnp.arange(HEAD_W)
    dd = lane % (HEAD_W // 2)
    axis = dd // (2 * freqs)
    half = (dd % (2 * freqs)) // freqs
    f = dd % freqs
    pos = jnp.where(axis[None, :] == 0, row[:, None], col[:, None])
    ang = pos * inv[f][None, :]
    cos = jnp.cos(ang)
    sin = jnp.sin(ang)
    s1 = jnp.where(half[None, :] == 0, -sin, 0.0)
    s2 = jnp.where(half[None, :] == 1, sin, 0.0)
    pad = lambda a, v: jnp.concatenate([a, jnp.full((n_ctx, HEAD_W), v, F32)], axis=0)
    return pad(cos, 1.0), pad(s1, 0.0), pad(s2, 0.0)


def _diff_lambda(lam_ref, lam_init):
    lv = lam_ref[...]
    return (jnp.exp(jnp.sum(lv[0:1, :] * lv[1:2, :], axis=1, keepdims=True))
            - jnp.exp(jnp.sum(lv[2:3, :] * lv[3:4, :], axis=1, keepdims=True)) + lam_init)


def _split_q(q_ref, hh):
    q = q_ref[:, hh * HEAD_W:(hh + 1) * HEAD_W].astype(F32)
    lane = lax.broadcasted_iota(I32, q.shape, 1)
    return jnp.concatenate([jnp.where(lane < HEAD_W // 2, q, 0.0),
                            jnp.where(lane >= HEAD_W // 2, q, 0.0)], axis=0).astype(BF16)


def _attn_finish(o_ref, sg_ref, hh, l8, acc, lam, lam_init):
    tq = o_ref.shape[0]
    r = 1.0 / jnp.sum(l8, axis=0, keepdims=True)
    ot = acc[:, :tq] * r[:, :tq] - lam * (acc[:, tq:] * r[:, tq:])
    msq = jnp.mean(ot * ot, axis=0, keepdims=True)
    ot = (ot * lax.rsqrt(msq + EPS)) * sg_ref[...] * (1.0 - lam_init)
    o_ref[:, hh * HEAD_W:(hh + 1) * HEAD_W] = ot.T.astype(BF16)


def _attn_pipe_kernel(lam_ref, sg_ref, q_ref, k_ref, vt_ref, o_ref, s0_scr, s1_scr, m0_scr, m1_scr,
                      acc_scr, *, lam_init):
    f = pl.program_id(1)
    tq = q_ref.shape[0]
    nh = q_ref.shape[1] // HEAD_W
    ng, _, gk = vt_ref.shape

    @pl.when(f == 0)
    def _():
        s1_scr[...] = jnp.zeros_like(s1_scr)
        m1_scr[...] = jnp.zeros_like(m1_scr)

    def step(sw_scr, mw_scr, sr_scr, mr_scr):
        lam = _diff_lambda(lam_ref, lam_init)
        qbd = [_split_q(q_ref, hh) for hh in range(nh)]
        m_prev = [mr_scr[hh] for hh in range(nh)]
        acc_scr[...] = jnp.zeros_like(acc_scr)

        def body(g, carry):
            m8s, l8s = carry
            base = pl.multiple_of(g * gk, gk)
            new_m8, new_l8 = [], []
            for hh in range(nh):
                cols = slice(hh * HEAD_W, (hh + 1) * HEAD_W)
                s = lax.dot_general(k_ref[pl.ds(base, gk), cols], qbd[hh],
                                    (((1,), (1,)), ((), ())), preferred_element_type=F32)
                sw_scr[hh, pl.ds(base, gk), :] = s
                new_m8.append(jnp.maximum(
                    m8s[hh], jnp.max(s.reshape(gk // SUBLANES, SUBLANES, 2 * tq), axis=0)))
                e = jnp.exp2(sr_scr[hh, pl.ds(base, gk), :] - m_prev[hh])
                new_l8.append(
                    l8s[hh] + jnp.sum(e.reshape(gk // SUBLANES, SUBLANES, 2 * tq), axis=0))
                acc_scr[hh] += jnp.dot(vt_ref[g, cols, :], e.astype(BF16),
                                       preferred_element_type=F32)
            return tuple(new_m8), tuple(new_l8)

        init = (tuple(jnp.full((SUBLANES, 2 * tq), -jnp.inf, F32) for _ in range(nh)),
                tuple(jnp.zeros((SUBLANES, 2 * tq), F32) for _ in range(nh)))
        m8s, l8s = init
        for g in range(ng):
            m8s, l8s = body(g, (m8s, l8s))
        for hh in range(nh):
            mw_scr[hh] = jnp.max(m8s[hh], axis=0, keepdims=True)
            _attn_finish(o_ref, sg_ref, hh, l8s[hh], acc_scr[hh], lam, lam_init)

    @pl.when(lax.rem(f, 2) == 0)
    def _():
        step(s0_scr, m0_scr, s1_scr, m1_scr)

    @pl.when(lax.rem(f, 2) == 1)
    def _():
        step(s1_scr, m1_scr, s0_scr, m0_scr)


def _attention_lat(qk, vt, lamv, sg, *, n_lat, tq, nh, lam_init):
    bsz, t, d2 = qk.shape
    d = d2 // 2
    hw = nh * HEAD_W
    heads = d // hw
    ng, gk = vt.shape[1], vt.shape[3]
    n_tiles = n_lat // tq
    last = heads * n_tiles - 1
    cur = lambda f: jnp.minimum(f, last)
    prev = lambda f: jnp.maximum(f - 1, 0)
    kern = functools.partial(_attn_pipe_kernel, lam_init=lam_init)
    return pl.pallas_call(
        kern,
        grid=(bsz, heads * n_tiles + 1),
        in_specs=[pl.BlockSpec((4, HEAD_W // 2), lambda b, f: (0, 0)),
                  pl.BlockSpec((HEAD_W, 1), lambda b, f: (0, 0)),
                  pl.BlockSpec((None, tq, hw), lambda b, f: (b, cur(f) % n_tiles, cur(f) // n_tiles)),
                  pl.BlockSpec((None, t, hw), lambda b, f: (b, 0, heads + cur(f) // n_tiles)),
                  pl.BlockSpec((None, ng, hw, gk), lambda b, f: (b, 0, prev(f) // n_tiles, 0))],
        out_specs=pl.BlockSpec((None, tq, hw),
                               lambda b, f: (b, prev(f) % n_tiles, prev(f) // n_tiles)),
        out_shape=jax.ShapeDtypeStruct((bsz, n_lat, d), BF16),
        scratch_shapes=[pltpu.VMEM((nh, t, 2 * tq), F32), pltpu.VMEM((nh, t, 2 * tq), F32),
                        pltpu.VMEM((nh, 1, 2 * tq), F32), pltpu.VMEM((nh, 1, 2 * tq), F32),
                        pltpu.VMEM((nh, HEAD_W, 2 * tq), F32)],
        compiler_params=_cparams(("arbitrary", "arbitrary")),
        name="diff_attn_lat",
    )(lamv, sg, qk, qk, vt)


def _attn_kernel(lam_ref, sg_ref, q_ref, k_ref, vt_ref, o_ref, s_scr, *, kc, v_off, lam_init):
    tq = q_ref.shape[0]
    tk = k_ref.shape[0]
    nh = q_ref.shape[1] // HEAD_W
    lam = _diff_lambda(lam_ref, lam_init)

    chunks = range(tk // kc)
    ms = []
    for hh in range(nh):
        qb = _split_q(q_ref, hh)
        m8 = jnp.full((SUBLANES, 2 * tq), -jnp.inf, F32)
        for c in chunks:
            s = lax.dot_general(k_ref[c * kc:(c + 1) * kc, hh * HEAD_W:(hh + 1) * HEAD_W], qb,
                                (((1,), (1,)), ((), ())), preferred_element_type=F32)
            s_scr[hh, c * kc:(c + 1) * kc, :] = s
            m8 = jnp.maximum(m8, jnp.max(s.reshape(kc // SUBLANES, SUBLANES, 2 * tq), axis=0))
        ms.append(jnp.max(m8, axis=0, keepdims=True))
    for hh in range(nh):
        l8 = jnp.zeros((SUBLANES, 2 * tq), F32)
        acc = jnp.zeros((HEAD_W, 2 * tq), F32)
        for c in chunks:
            e = jnp.exp2(s_scr[hh, c * kc:(c + 1) * kc, :] - ms[hh])
            l8 = l8 + jnp.sum(e.reshape(kc // SUBLANES, SUBLANES, 2 * tq), axis=0)
            acc = acc + jnp.dot(
                vt_ref[hh * HEAD_W:(hh + 1) * HEAD_W, v_off + c * kc:v_off + (c + 1) * kc],
                e.astype(BF16), preferred_element_type=F32)
        _attn_finish(o_ref, sg_ref, hh, l8, acc, lam, lam_init)


def _attention_ctx(qk, vt, lamv, sg, *, n_lat, nh, lam_init):
    bsz, t, d2 = qk.shape
    d = d2 // 2
    hw = nh * HEAD_W
    heads = d // hw
    n_ctx = t - n_lat
    gk = vt.shape[3]
    assert n_lat % n_ctx == 0 and n_lat % gk + n_ctx <= gk
    rb = n_lat // n_ctx
    kern = functools.partial(_attn_kernel, kc=n_ctx, v_off=n_lat % gk, lam_init=lam_init)
    return pl.pallas_call(
        kern,
        grid=(bsz, heads),
        in_specs=[pl.BlockSpec((4, HEAD_W // 2), lambda b, h: (0, 0)),
                  pl.BlockSpec((HEAD_W, 1), lambda b, h: (0, 0)),
                  pl.BlockSpec((None, n_ctx, hw), lambda b, h: (b, rb, h)),
                  pl.BlockSpec((None, n_ctx, hw), lambda b, h: (b, rb, heads + h)),
                  pl.BlockSpec((None, None, hw, gk), lambda b, h: (b, n_lat // gk, h, 0))],
        out_specs=pl.BlockSpec((None, n_ctx, hw), lambda b, h: (b, 0, h)),
        out_shape=jax.ShapeDtypeStruct((bsz, n_ctx, d), BF16),
        scratch_shapes=[pltpu.VMEM((nh, n_ctx, 2 * n_ctx), F32)],
        compiler_params=_cparams(("arbitrary", "arbitrary")),
        name="diff_attn_ctx",
    )(lamv, sg, qk, qk, vt)


def _router_probs(m, wr, n_exp):
    m_hi = m.astype(BF16)
    m_lo = (m - m_hi.astype(F32)).astype(BF16)
    w_hi = wr.astype(BF16)
    w_lo = (wr - w_hi.astype(F32)).astype(BF16)
    logits = (jnp.dot(m_hi, w_hi, preferred_element_type=F32)
              + jnp.dot(m_lo, w_hi, preferred_element_type=F32)
              + jnp.dot(m_hi, w_lo, preferred_element_type=F32))
    lane = lax.broadcasted_iota(I32, logits.shape, 1)
    lg = jnp.where(lane < n_exp, logits, -jnp.inf)
    ex = jnp.exp(lg - jnp.max(lg, axis=1, keepdims=True))
    return m_hi, ex / jnp.sum(ex, axis=1, keepdims=True)


def _proj_res_kernel(u_ref, w_ref, h_ref, gate_ref, o_ref, *, n_lat, ctx_row, nc):
    b = pl.program_id(0)
    i = pl.program_id(1)
    tt, d = h_ref.shape
    gate = _row_mod(gate_ref, b, ctx_row, i * tt, tt, n_lat)
    u = u_ref[...]
    for n in range(d // nc):
        y = jnp.dot(u, w_ref[:, n * nc:(n + 1) * nc], preferred_element_type=F32)
        o_ref[:, n * nc:(n + 1) * nc] = (h_ref[:, n * nc:(n + 1) * nc]
                                         + gate[:, n * nc:(n + 1) * nc] * y)


def _proj_res(u, w, h, mod, *, gate_chunk, n_rows, n_lat, tt):
    bsz, _, k = u.shape
    d = h.shape[-1]
    kern = functools.partial(_proj_res_kernel, n_lat=n_lat, ctx_row=bsz, nc=256)
    return pl.pallas_call(
        kern,
        grid=(bsz, n_rows // tt),
        in_specs=[pl.BlockSpec((None, tt, k), lambda b, i: (b, i, 0)),
                  pl.BlockSpec((k, d), lambda b, i: (0, 0)),
                  pl.BlockSpec((None, tt, d), lambda b, i: (b, i, 0)),
                  pl.BlockSpec((MOD_ROWS, d), lambda b, i: (0, gate_chunk))],
        out_specs=pl.BlockSpec((None, tt, d), lambda b, i: (b, i, 0)),
        out_shape=jax.ShapeDtypeStruct((bsz, n_rows, d), F32),
        compiler_params=_cparams(("arbitrary", "arbitrary")),
        name="proj_residual",
    )(u, w, h, mod)


def _norm_probs_kernel(h_ref, g_ref, sh_ref, sc_ref, wr_ref, m_ref, p_ref, *,
                       n_lat, ctx_row, n_exp):
    b = pl.program_id(0)
    i = pl.program_id(1)
    tt, d = h_ref.shape
    shift = _row_mod(sh_ref, b, ctx_row, i * tt, tt, n_lat)
    scale = _row_mod(sc_ref, b, ctx_row, i * tt, tt, n_lat)
    m = _ln_mod(h_ref[...], g_ref[...], shift, scale)
    m_ref[...], p_ref[...] = _router_probs(m, wr_ref[...], n_exp)


def _norm_probs(h, g, mod, wr, *, n_lat, n_exp, tt):
    bsz, t, d = h.shape
    kern = functools.partial(_norm_probs_kernel, n_lat=n_lat, ctx_row=bsz, n_exp=n_exp)
    return pl.pallas_call(
        kern,
        grid=(bsz, t // tt),
        in_specs=[pl.BlockSpec((None, tt, d), lambda b, i: (b, i, 0)),
                  pl.BlockSpec((1, d), lambda b, i: (0, 0)),
                  pl.BlockSpec((MOD_ROWS, d), lambda b, i: (0, 3)),
                  pl.BlockSpec((MOD_ROWS, d), lambda b, i: (0, 4)),
                  pl.BlockSpec((d, LANES), lambda b, i: (0, 0))],
        out_specs=[pl.BlockSpec((None, tt, d), lambda b, i: (b, i, 0)),
                   pl.BlockSpec((None, tt, LANES), lambda b, i: (b, i, 0))],
        out_shape=[jax.ShapeDtypeStruct((bsz, t, d), BF16),
                   jax.ShapeDtypeStruct((bsz, t, LANES), F32)],
        compiler_params=_cparams(("arbitrary", "arbitrary")),
        name="moe_norm_probs",
    )(h, g, mod, mod, wr)


def _lru_in_kernel(h_ref, g_ref, sh_ref, sc_ref, w_ref, bias_ref, gy_ref, xr_ref, *,
                   n_lat, ctx_row, nc):
    b = pl.program_id(0)
    i = pl.program_id(1)
    tt, d = h_ref.shape
    shift = _row_mod(sh_ref, b, ctx_row, i * tt, tt, n_lat)
    scale = _row_mod(sc_ref, b, ctx_row, i * tt, tt, n_lat)
    xn = _ln_mod(h_ref[...], g_ref[...], shift, scale).astype(BF16)
    for n in range(d // nc):
        y = jnp.dot(xn, w_ref[:, n * nc:(n + 1) * nc], preferred_element_type=F32)
        y = y + bias_ref[:, n * nc:(n + 1) * nc]
        gy_ref[:, n * nc:(n + 1) * nc] = _gelu_tanh(y).astype(BF16)
    for n in range(d // nc):
        x = jnp.dot(xn, w_ref[:, d + n * nc:d + (n + 1) * nc], preferred_element_type=F32)
        xr_ref[:, n * nc:(n + 1) * nc] = x + bias_ref[:, d + n * nc:d + (n + 1) * nc]


def _lru_in(h, g, mod, w_in, b_in, *, n_lat, tt):
    bsz, t, d = h.shape
    kern = functools.partial(_lru_in_kernel, n_lat=n_lat, ctx_row=bsz, nc=256)
    return pl.pallas_call(
        kern,
        grid=(bsz, t // tt),
        in_specs=[pl.BlockSpec((None, tt, d), lambda b, i: (b, i, 0)),
                  pl.BlockSpec((1, d), lambda b, i: (0, 0)),
                  pl.BlockSpec((MOD_ROWS, d), lambda b, i: (0, 0)),
                  pl.BlockSpec((MOD_ROWS, d), lambda b, i: (0, 1)),
                  pl.BlockSpec((d, 2 * d), lambda b, i: (0, 0)),
                  pl.BlockSpec((1, 2 * d), lambda b, i: (0, 0))],
        out_specs=[pl.BlockSpec((None, tt, d), lambda b, i: (b, i, 0)),
                   pl.BlockSpec((None, tt, d), lambda b, i: (b, i, 0))],
        out_shape=[jax.ShapeDtypeStruct((bsz, t, d), BF16),
                   jax.ShapeDtypeStruct((bsz, t, d), F32)],
        compiler_params=_cparams(("arbitrary", "arbitrary")),
        name="lru_in_proj",
    )(h, g, mod, mod, w_in, b_in)


def _lru_kernel(xr_ref, gy_ref, cw_ref, cb_ref, wg_ref, bg_ref, lam_ref, u_ref,
                xp_scr, hf_scr, hb_scr, *, n_lat):
    t, cw = xr_ref.shape
    n_ctx = t - n_lat
    seg, pitch, nsub = LRU_SEG, LRU_PITCH, SUBLANES
    rc = seg * nsub
    nseg_lat, nseg_ctx = n_lat // seg, n_ctx // seg
    lat0, ctx0 = 1, nseg_lat + 2
    zseg = jnp.zeros((seg, cw), F32)
    for s in (0, nseg_lat + 1, nseg_lat + nseg_ctx + 2):
        xp_scr[s * pitch:s * pitch + seg, :] = zseg
    for s in range(nseg_lat):
        xp_scr[(lat0 + s) * pitch:(lat0 + s) * pitch + seg, :] = xr_ref[s * seg:(s + 1) * seg, :]
    for s in range(nseg_ctx):
        xp_scr[(ctx0 + s) * pitch:(ctx0 + s) * pitch + seg, :] = (
            xr_ref[n_lat + s * seg:n_lat + (s + 1) * seg, :])
    lam = lam_ref[...]
    sp = jnp.maximum(-lam, 0.0) + jnp.log(1.0 + jnp.exp(-jnp.abs(lam)))
    w = cw_ref[...]
    cb = cb_ref[...]

    def rows8(ref, start):
        return ref[pl.ds(start, nsub, stride=pitch), :]

    def chunk(pb, carry, d, reverse, out_scr):
        x = [rows8(xp_scr, pb + g) for g in range(seg)]
        xm1 = [rows8(xp_scr, pb - (pitch - seg) - 1)] + x[:-1]
        xm2 = [rows8(xp_scr, pb - (pitch - seg) - 2), xm1[0]] + x[:-2]
        xp1 = x[1:] + [rows8(xp_scr, pb + pitch)]
        xc = jnp.concatenate(
            [cb + w[0:1, :] * xm2[g] + w[1:2, :] * xm1[g] + w[2:3, :] * x[g] + w[3:4, :] * xp1[g]
             for g in range(seg)], axis=0)
        gpre = jnp.dot(xc.astype(BF16), wg_ref[d, 0], preferred_element_type=F32) + bg_ref[d, 0]
        r = _sigmoid(gpre[:, :LRU_BW])
        ig = _sigmoid(gpre[:, LRU_BW:])
        a = jnp.exp((-LRU_C * sp[d:d + 1, :]) * r)
        bt = jnp.sqrt(1.0 - a * a) * (ig * xc)
        a3 = a.reshape(seg, nsub, cw)
        b3 = bt.reshape(seg, nsub, cw)
        h = jnp.zeros((nsub, cw), F32)
        p = jnp.ones((nsub, cw), F32)
        hl, pl_ = [None] * seg, [None] * seg
        for g in (range(seg - 1, -1, -1) if reverse else range(seg)):
            h = a3[g] * h + b3[g]
            p = a3[g] * p
            hl[g], pl_[g] = h, p
        entry = [None] * nsub
        for j in (range(nsub - 1, -1, -1) if reverse else range(nsub)):
            entry[j] = carry
            carry = h[j:j + 1, :] + p[j:j + 1, :] * carry
        h_in = jnp.concatenate(entry, axis=0)
        for g in range(seg):
            out_scr[pl.ds(pb + g, nsub, stride=pitch), :] = hl[g] + pl_[g] * h_in
        return carry

    lat_pb = lambda c: lat0 * pitch + c * (nsub * pitch)
    ctx_pb = lambda c: (ctx0 + c * nsub) * pitch
    n_lat_chunks, n_ctx_chunks = n_lat // rc, n_ctx // rc
    cf = cb_ = jnp.zeros((1, cw), F32)
    for c in range(n_ctx_chunks):
        cf = chunk(ctx_pb(c), cf, 0, False, hf_scr)
        cb_ = chunk(ctx_pb(n_ctx_chunks - 1 - c), cb_, 1, True, hb_scr)

    def both(c, carries):
        return (chunk(lat_pb(c), carries[0], 0, False, hf_scr),
                chunk(lat_pb(n_lat_chunks - 1 - c), carries[1], 1, True, hb_scr))

    lax.fori_loop(0, n_lat_chunks, both, (cf, cb_))
    for s in range(nseg_lat):
        rows = slice((lat0 + s) * pitch, (lat0 + s) * pitch + seg)
        u_ref[s * seg:(s + 1) * seg, :] = (gy_ref[s * seg:(s + 1) * seg, :].astype(F32)
                                           * (hf_scr[rows, :] + hb_scr[rows, :])).astype(BF16)


def _lru_core(xr, gy, conv_w, conv_b, w_gates, b_gates, lam, *, n_lat):
    bsz, t, d = xr.shape
    cw = LRU_BW
    rows = ((t // LRU_SEG) + 3) * LRU_PITCH
    kern = functools.partial(_lru_kernel, n_lat=n_lat)
    return pl.pallas_call(
        kern,
        grid=(bsz, d // cw),
        in_specs=[pl.BlockSpec((None, t, cw), lambda b, k: (b, 0, k)),
                  pl.BlockSpec((None, n_lat, cw), lambda b, k: (b, 0, k)),
                  pl.BlockSpec((CONV_W, cw), lambda b, k: (0, k)),
                  pl.BlockSpec((1, cw), lambda b, k: (0, k)),
                  pl.BlockSpec((2, 1, LRU_BW, 2 * LRU_BW), lambda b, k: (0, k, 0, 0)),
                  pl.BlockSpec((2, 1, 1, 2 * LRU_BW), lambda b, k: (0, k, 0, 0)),
                  pl.BlockSpec((2, cw), lambda b, k: (0, k))],
        out_specs=pl.BlockSpec((None, n_lat, cw), lambda b, k: (b, 0, k)),
        out_shape=jax.ShapeDtypeStruct((bsz, n_lat, d), BF16),
        scratch_shapes=[pltpu.VMEM((rows, cw), F32)] * 3,
        compiler_params=_cparams(("arbitrary", "arbitrary")),
        name="lru_core",
    )(xr, gy, conv_w, conv_b, w_gates, b_gates, lam)


def _select(p, cap, n_exp):
    n_tok = p.shape[0]
    n_rows = -(-n_exp // SUBLANES) * SUBLANES
    pt = p.T[0:n_rows, :]

    def body(i, thr):
        cand = thr | jnp.left_shift(jnp.int32(1), 29 - i)
        hit = jnp.where(pt >= lax.bitcast_convert_type(cand, F32), 1.0, 0.0)
        return jnp.where(jnp.sum(hit, axis=1, keepdims=True) >= cap, cand, thr)

    thr_col = lax.fori_loop(0, 30, body, jnp.zeros((n_rows, 1), I32))
    diag = (lax.broadcasted_iota(I32, (n_rows, LANES), 0)
            == lax.broadcasted_iota(I32, (n_rows, LANES), 1))
    thr = jnp.sum(jnp.where(diag, jnp.broadcast_to(thr_col, (n_rows, LANES)), 0),
                  axis=0, keepdims=True)
    gt = p >= pltpu.bitcast(thr + 1, F32)
    eq = (p >= pltpu.bitcast(thr, F32)) & jnp.logical_not(gt)
    blk = PREFIX_BLOCK
    ltri = jnp.where(lax.broadcasted_iota(I32, (blk, blk), 1)
                     < lax.broadcasted_iota(I32, (blk, blk), 0), 1.0, 0.0).astype(BF16)
    masks = jnp.concatenate([jnp.where(gt, 1.0, 0.0), jnp.where(eq, 1.0, 0.0)], axis=1)
    off = jnp.zeros((1, 2 * LANES), F32)
    pres = []
    for i in range(n_tok // blk):
        mb = masks[i * blk:(i + 1) * blk, :]
        pre = jnp.dot(ltri, mb.astype(BF16), preferred_element_type=F32) + off
        pres.append(pre)
        off = pre[blk - 1:blk, :] + mb[blk - 1:blk, :]
    pre = jnp.concatenate(pres, axis=0)
    pre_gt, pre_eq = pre[:, :LANES], pre[:, LANES:]
    need = cap - off[:, :LANES]
    sel = gt | (eq & (pre_eq < need))
    slot = pre_gt + jnp.minimum(pre_eq, need)
    bounds = jnp.concatenate([slot[i * blk:i * blk + 1, :] for i in range(n_tok // blk)]
                             + [jnp.full((1, LANES), float(cap), F32)], axis=0)
    return jnp.where(sel, slot, -1.0), jnp.where(sel, p, 0.0), bounds


def _select_kernel(p_ref, slot_ref, gate_ref, slot_t_ref, gate_t_ref, bounds_ref, *,
                   n_lat, n_exp):
    t = p_ref.shape[0]
    n_ctx = t - n_lat
    slot, gate, bounds = _select(p_ref[0:n_lat, :], EC_CAPACITY * n_lat // n_exp, n_exp)
    slot_ref[0:n_lat, :] = slot
    gate_ref[0:n_lat, :] = gate
    slot_t_ref[:, 0:n_lat] = slot.T
    gate_t_ref[:, 0:n_lat] = gate.T
    bounds_ref[...] = jnp.zeros_like(bounds_ref)
    bounds_ref[0:bounds.shape[0], :] = bounds
    if n_ctx:
        slot, gate, _ = _select(p_ref[n_lat:t, :], EC_CAPACITY * n_ctx // n_exp, n_exp)
        slot_ref[n_lat:t, :] = slot
        gate_ref[n_lat:t, :] = gate
        slot_t_ref[:, n_lat:t] = slot.T
        gate_t_ref[:, n_lat:t] = gate.T


def _select_tokens(p, *, n_lat, n_exp):
    bsz, t, _ = p.shape
    nb_rows = -(-(n_lat // PREFIX_BLOCK + 1) // SUBLANES) * SUBLANES
    kern = functools.partial(_select_kernel, n_lat=n_lat, n_exp=n_exp)
    return pl.pallas_call(
        kern,
        grid=(bsz,),
        in_specs=[pl.BlockSpec((None, t, LANES), lambda b: (b, 0, 0))],
        out_specs=[pl.BlockSpec((None, t, LANES), lambda b: (b, 0, 0)),
                   pl.BlockSpec((None, t, LANES), lambda b: (b, 0, 0)),
                   pl.BlockSpec((None, LANES, t), lambda b: (b, 0, 0)),
                   pl.BlockSpec((None, LANES, t), lambda b: (b, 0, 0)),
                   pl.BlockSpec((None, nb_rows, LANES), lambda b: (b, 0, 0))],
        out_shape=[jax.ShapeDtypeStruct((bsz, t, LANES), F32),
                   jax.ShapeDtypeStruct((bsz, t, LANES), F32),
                   jax.ShapeDtypeStruct((bsz, LANES, t), F32),
                   jax.ShapeDtypeStruct((bsz, LANES, t), F32),
                   jax.ShapeDtypeStruct((bsz, nb_rows, LANES), F32)],
        compiler_params=_cparams(("arbitrary",)),
        name="moe_select",
    )(p)


def _gather_kernel(bounds_ref, m_ref, slot_t_ref, xg_ref, acc_scr, *, n_lat, cap_lat, cap_ctx):
    b = pl.program_id(0)
    kb = pl.program_id(1)
    n_exp = xg_ref.shape[0]
    blk, d = m_ref.shape
    n_lat_blocks = n_lat // blk

    @pl.when(kb == 0)
    def _():
        acc_scr[...] = jnp.zeros_like(acc_scr)

    base = (b * (n_lat_blocks + 1) + jnp.minimum(kb, n_lat_blocks - 1)) * n_exp
    los = [bounds_ref[base + e] for e in range(n_exp)]
    his = [bounds_ref[base + n_exp + e] for e in range(n_exp)]
    taken = kb >= n_lat_blocks

    def one_hot(rows, first):
        sl = lax.broadcasted_iota(I32, (rows, 1), 0).astype(F32)
        return [jnp.where(slot_t_ref[e:e + 1, :] - first[e] == sl, 1.0, 0.0).astype(BF16)
                for e in range(n_exp)]

    def window_path(win, starts):
        p = jnp.concatenate(one_hot(win, [s.astype(F32) for s in starts]), axis=0)
        res = jnp.dot(p, m_ref[...], preferred_element_type=F32)
        for e in range(n_exp):
            acc_scr[e, pl.ds(starts[e], win), :] += res[e * win:(e + 1) * win, :]

    for win in (64, 128):
        if win > cap_lat:
            continue
        starts, fits = [], None
        for e in range(n_exp):
            st = jnp.minimum(lax.shift_left(lax.shift_right_logical(los[e], 4), 4),
                             cap_lat - win)
            starts.append(pl.multiple_of(st, 16))
            ok = his[e] - st <= win
            fits = ok if fits is None else jnp.logical_and(fits, ok)
        pl.when(jnp.logical_and(jnp.logical_not(taken), fits))(
            functools.partial(window_path, win, starts))
        taken = jnp.logical_or(taken, fits)

    @pl.when(jnp.logical_not(taken))
    def _():
        p = jnp.concatenate(one_hot(cap_lat, [0.0] * n_exp), axis=0)
        res = jnp.dot(p, m_ref[...], preferred_element_type=F32)
        for e in range(n_exp):
            acc_scr[e] += res[e * cap_lat:(e + 1) * cap_lat, :]

    @pl.when(kb == n_lat_blocks - 1)
    def _():
        for e in range(n_exp):
            xg_ref[e, 0:cap_lat, :] = acc_scr[e].astype(BF16)

    if cap_ctx:
        @pl.when(kb >= n_lat_blocks)
        def _():
            p = jnp.concatenate(one_hot(cap_ctx, [0.0] * n_exp), axis=0)
            res = jnp.dot(p, m_ref[...], preferred_element_type=F32).astype(BF16)
            for e in range(n_exp):
                xg_ref[e, cap_lat:cap_lat + cap_ctx, :] = res[e * cap_ctx:(e + 1) * cap_ctx, :]


def _gather(m, slot_t, table, *, n_lat, n_exp):
    bsz, t, d = m.shape
    blk = PREFIX_BLOCK
    cap_lat = EC_CAPACITY * n_lat // n_exp
    cap_ctx = EC_CAPACITY * (t - n_lat) // n_exp
    assert t - n_lat in (0, blk)
    r = cap_lat + cap_ctx
    kern = functools.partial(_gather_kernel, n_lat=n_lat, cap_lat=cap_lat, cap_ctx=cap_ctx)
    grid_spec = pltpu.PrefetchScalarGridSpec(
        num_scalar_prefetch=1,
        grid=(bsz, t // blk),
        in_specs=[pl.BlockSpec((None, blk, d), lambda b, kb, tbl: (b, kb, 0)),
                  pl.BlockSpec((None, LANES, blk), lambda b, kb, tbl: (b, 0, kb))],
        out_specs=pl.BlockSpec((n_exp, None, r, d), lambda b, kb, tbl: (0, b, 0, 0)),
        scratch_shapes=[pltpu.VMEM((n_exp, cap_lat, d), F32)])
    return pl.pallas_call(
        kern,
        grid_spec=grid_spec,
        out_shape=jax.ShapeDtypeStruct((n_exp, bsz, r, d), BF16),
        compiler_params=_cparams(("arbitrary", "arbitrary")),
        name="moe_gather",
    )(table, m, slot_t)


def _ffn_kernel(x_ref, wg_ref, wu_ref, wd_ref, y_ref, acc_scr, *, rc):
    j = pl.program_id(1)
    nj = pl.num_programs(1)
    rows = x_ref.shape[0]

    @pl.when(j == 0)
    def _():
        acc_scr[...] = jnp.zeros_like(acc_scr)

    wg = wg_ref[...].astype(BF16)
    wu = wu_ref[...].astype(BF16)
    wd = wd_ref[...].astype(BF16)

    for c in range(rows // rc):
        x = x_ref[c * rc:(c + 1) * rc, :]
        g = jnp.dot(x, wg, preferred_element_type=F32)
        u = jnp.dot(x, wu, preferred_element_type=F32)
        hid = ((g * _sigmoid(g)) * u).astype(BF16)
        acc_scr[c * rc:(c + 1) * rc, :] += jnp.dot(hid, wd, preferred_element_type=F32)

    @pl.when(j == nj - 1)
    def _():
        y_ref[...] = acc_scr[...].astype(BF16)


def _ffn(xg, w_gate_up, w_down, *, layer, fc, rc):
    n_exp, rows, d = xg.shape
    f = w_down.shape[2]
    nj = f // fc
    kern = functools.partial(_ffn_kernel, rc=rc)
    return pl.pallas_call(
        kern,
        grid=(n_exp, nj),
        in_specs=[pl.BlockSpec((None, rows, d), lambda e, j: (e, 0, 0)),
                  pl.BlockSpec((None, None, d, fc), lambda e, j: (layer, e, 0, j)),
                  pl.BlockSpec((None, None, d, fc), lambda e, j: (layer, e, 0, nj + j)),
                  pl.BlockSpec((None, None, fc, d), lambda e, j: (layer, e, j, 0))],
        out_specs=pl.BlockSpec((None, rows, d), lambda e, j: (e, 0, 0)),
        out_shape=jax.ShapeDtypeStruct((n_exp, rows, d), BF16),
        scratch_shapes=[pltpu.VMEM((rows, d), F32)],
        compiler_params=_cparams(("arbitrary", "arbitrary")),
        name="moe_ffn",
    )(xg, w_gate_up, w_gate_up, w_down)


def _combine_kernel(bounds_ref, y_ref, slot_ref, gate_ref, slot_t_ref, gate_t_ref, h_ref, g2_ref,
                    fg_ref, o_ref, *, n_lat, ctx_row, cap_lat, cap_ctx, final_norm):
    b = pl.program_id(0)
    i = pl.program_id(1)
    tt, d = h_ref.shape
    n_exp = y_ref.shape[0]
    n_lat_tiles = n_lat // tt

    def finish(acc):
        gate2 = _row_mod(g2_ref, b, ctx_row, i * tt, tt, n_lat)
        out = h_ref[...] + gate2 * acc
        if final_norm:
            ms = jnp.mean(out * out, axis=-1, keepdims=True)
            out = (out * lax.rsqrt(ms + EPS)) * fg_ref[...]
        o_ref[...] = out

    base = (b * (n_lat_tiles + 1) + jnp.minimum(i, n_lat_tiles - 1)) * n_exp
    los = [bounds_ref[base + e] for e in range(n_exp)]
    his = [bounds_ref[base + n_exp + e] for e in range(n_exp)]
    taken = i >= n_lat_tiles

    def window_path(win, starts):
        sl = lax.broadcasted_iota(I32, (win, 1), 0).astype(F32)
        ps, ys = [], []
        for e in range(n_exp):
            rel = slot_t_ref[e:e + 1, :] - starts[e].astype(F32)
            ps.append(jnp.where(rel == sl, gate_t_ref[e:e + 1, :], 0.0).astype(BF16))
            ys.append(y_ref[e, pl.ds(starts[e], win), :])
        finish(lax.dot_general(jnp.concatenate(ps, axis=0), jnp.concatenate(ys, axis=0),
                               (((0,), (0,)), ((), ())), preferred_element_type=F32))

    for win in (64, 128):
        if win > cap_lat:
            continue
        starts, fits = [], None
        for e in range(n_exp):
            st = jnp.minimum(lax.shift_left(lax.shift_right_logical(los[e], 4), 4),
                             cap_lat - win)
            starts.append(pl.multiple_of(st, 16))
            ok = his[e] - st <= win
            fits = ok if fits is None else jnp.logical_and(fits, ok)
        pl.when(jnp.logical_and(jnp.logical_not(taken), fits))(
            functools.partial(window_path, win, starts))
        taken = jnp.logical_or(taken, fits)

    @pl.when(jnp.logical_not(taken))
    def _():
        slot = slot_ref[...]
        gate = gate_ref[...]
        iota_c = lax.broadcasted_iota(I32, (1, cap_lat), 1).astype(F32)
        acc = jnp.zeros((tt, d), F32)
        for e in range(n_exp):
            pt = jnp.where(slot[:, e:e + 1] == iota_c, gate[:, e:e + 1], 0.0).astype(BF16)
            acc = acc + jnp.dot(pt, y_ref[e, 0:cap_lat, :], preferred_element_type=F32)
        finish(acc)

    if cap_ctx:
        @pl.when(i >= n_lat_tiles)
        def _():
            slot = slot_ref[...]
            gate = gate_ref[...]
            iota_c = lax.broadcasted_iota(I32, (1, n_exp * cap_ctx), 1).astype(F32)
            pt = jnp.zeros((tt, n_exp * cap_ctx), F32)
            for e in range(n_exp):
                pt = pt + jnp.where(slot[:, e:e + 1] + float(e * cap_ctx) == iota_c,
                                    gate[:, e:e + 1], 0.0)
            yc = y_ref[:, cap_lat:cap_lat + cap_ctx, :].reshape(n_exp * cap_ctx, d)
            finish(jnp.dot(pt.astype(BF16), yc, preferred_element_type=F32))


def _combine(y, slot, gate, slot_t, gate_t, table, h, mod, final_g, *, n_lat, n_out, final_norm):
    n_exp, bsz, r, d = y.shape
    t = slot.shape[1]
    tt = PREFIX_BLOCK
    cap_lat = EC_CAPACITY * n_lat // n_exp
    cap_ctx = EC_CAPACITY * (t - n_lat) // n_exp
    kern = functools.partial(_combine_kernel, n_lat=n_lat, ctx_row=bsz, cap_lat=cap_lat,
                             cap_ctx=cap_ctx, final_norm=final_norm)
    grid_spec = pltpu.PrefetchScalarGridSpec(
        num_scalar_prefetch=1,
        grid=(bsz, n_out // tt),
        in_specs=[pl.BlockSpec((n_exp, None, r, d), lambda b, i, tbl: (0, b, 0, 0)),
                  pl.BlockSpec((None, tt, LANES), lambda b, i, tbl: (b, i, 0)),
                  pl.BlockSpec((None, tt, LANES), lambda b, i, tbl: (b, i, 0)),
                  pl.BlockSpec((None, LANES, tt), lambda b, i, tbl: (b, 0, i)),
                  pl.BlockSpec((None, LANES, tt), lambda b, i, tbl: (b, 0, i)),
                  pl.BlockSpec((None, tt, d), lambda b, i, tbl: (b, i, 0)),
                  pl.BlockSpec((MOD_ROWS, d), lambda b, i, tbl: (0, 5)),
                  pl.BlockSpec((1, d), lambda b, i, tbl: (0, 0))],
        out_specs=pl.BlockSpec((None, tt, d), lambda b, i, tbl: (b, i, 0)))
    return pl.pallas_call(
        kern,
        grid_spec=grid_spec,
        out_shape=jax.ShapeDtypeStruct((bsz, n_out, d), F32),
        compiler_params=_cparams(("arbitrary", "arbitrary")),
        name="moe_combine",
    )(table, y, slot, gate, slot_t, gate_t, h, mod, final_g)


def _moe(h, m, p, mod, w_gate_up, w_down, final_g, *, layer, n_lat, n_exp, final_norm):
    bsz, t, d = h.shape
    slot, gate, slot_t, gate_t, bounds = _select_tokens(p, n_lat=n_lat, n_exp=n_exp)
    table = bounds[:, :n_lat // PREFIX_BLOCK + 1, :n_exp].astype(I32).reshape(-1)
    xg = _gather(m, slot_t, table, n_lat=n_lat, n_exp=n_exp)
    r = xg.shape[2]
    y = _ffn(xg.reshape(n_exp, bsz * r, d), w_gate_up, w_down, layer=layer, fc=256,
             rc=_pick(bsz * r, (768, 1024, 512, 256, 128, 64, 32, 16)))
    return _combine(y.reshape(n_exp, bsz, r, d), slot, gate, slot_t, gate_t, table, h, mod, final_g,
                    n_lat=n_lat, n_out=t, final_norm=final_norm)


def kernel(x, c, ctx, c_ctx, ada_w, ada_b, norm1_g, norm2_g, final_g, attn_w_qkv, attn_lq1, attn_lk1, attn_lq2, attn_lk2, attn_subln_g, attn_w_o, lru_w_in, lru_b_in, lru_conv_w, lru_conv_b, lru_w_gates, lru_b_gates, lru_lambda, lru_w_out, moe_w_router, moe_w_gate_up, moe_w_down):
    bsz, n_lat, d = x.shape
    n_ctx = ctx.shape[1]
    depth = ada_w.shape[0]
    assert bsz < MOD_ROWS and d % (2 * HEAD_W) == 0 and n_lat % GRID_W == 0

    cc = jnp.concatenate([c, c_ctx[None, :], jnp.zeros((MOD_ROWS - bsz - 1, d), F32)], axis=0)
    mods = _mod_tables(cc, ada_w, ada_b)
    h = jnp.concatenate([x, ctx], axis=1)
    row = lambda v: v.reshape(1, -1)

    for i in range(depth):
        last = i == depth - 1
        mod = mods[i]
        j = i // N_MIXERS
        n_exp = moe_w_router.shape[-1]
        wr = jnp.pad(moe_w_router[i], ((0, 0), (0, LANES - n_exp)))
        if i % N_MIXERS == 0:
            lam_init = 0.8 - 0.6 * math.exp(-0.3 * i)
            w = attn_w_qkv[j]
            wqk = w[:, :2 * d].astype(BF16)
            wvt = w[:, 2 * d:].T.astype(BF16)
            cos, s1, s2 = _rope_tables(n_lat, n_ctx)
            t_all = n_lat + n_ctx
            qk, vt = _qkv(h, row(norm1_g[i]), mod, wqk, wvt, cos, s1, s2, n_lat=n_lat,
                          tt=_pick(t_all, (768, 256)), vs=1)
            lamv = jnp.stack([attn_lq1[j], attn_lk1[j], attn_lq2[j], attn_lk2[j]], axis=0)
            sg = attn_subln_g[j].reshape(HEAD_W, 1)
            n_rows = n_lat if last else n_lat + n_ctx
            o = _attention_lat(qk, vt, lamv, sg, n_lat=n_lat, tq=256, nh=2, lam_init=lam_init)
            if not last:
                o_ctx = _attention_ctx(qk, vt, lamv, sg, n_lat=n_lat, nh=2, lam_init=lam_init)
                o = jnp.concatenate([o, o_ctx], axis=1)
            h = _proj_res(o, attn_w_o[j].astype(BF16), h, mod, gate_chunk=2, n_rows=n_rows,
                          n_lat=n_lat, tt=_pick(n_rows, (768, 512, 256)))
        else:
            gy, xr = _lru_in(h, row(norm1_g[i]), mod, lru_w_in[j].astype(BF16),
                             row(lru_b_in[j]), n_lat=n_lat, tt=_pick(n_lat + n_ctx, (768, 256)))
            nblk = d // LRU_BW
            u = _lru_core(xr, gy, lru_conv_w[j], row(lru_conv_b[j]),
                          lru_w_gates[j].astype(BF16),
                          lru_b_gates[j].reshape(2, nblk, 1, 2 * LRU_BW), lru_lambda[j],
                          n_lat=n_lat)
            n_rows = n_lat if last else n_lat + n_ctx
            assert last, "context output of the recurrent mixer is only needed in non-final layers"
            h = _proj_res(u, lru_w_out[j].astype(BF16), h, mod, gate_chunk=2, n_rows=n_rows,
                          n_lat=n_lat, tt=_pick(n_rows, (1024, 512, 256)))
        m, p = _norm_probs(h, row(norm2_g[i]), mod, wr, n_lat=n_lat, n_exp=n_exp,
                           tt=_pick(h.shape[1], (768, 512, 256)))
        h = _moe(h, m, p, mod, moe_w_gate_up, moe_w_down, row(final_g), layer=i, n_lat=n_lat,
                 n_exp=n_exp, final_norm=last)
    return h
```

```python
import functools
import math

import jax
import jax.numpy as jnp
from jax import lax
from jax.experimental import pallas as pl
from jax.experimental.pallas import tpu as pltpu

F32 = jnp.float32
BF16 = jnp.bfloat16
I32 = jnp.int32

EPS = 1e-6
GRID_W = 64
ROPE_THETA = 10000.0
LRU_C = 8.0
EC_CAPACITY = 2
N_MIXERS = 2
HEAD_W = 128
LRU_BW = 128
LRU_SEG = 32
LRU_PITCH = 40
CONV_W = 4
MOD_ROWS = 16
PREFIX_BLOCK = 256
LANES = 128
SUBLANES = 8
VMEM_LIMIT = 56 * 2**20


def _pick(n, cands):
    return next(c for c in cands if n % c == 0)


def _cparams(sem):
    return pltpu.CompilerParams(dimension_semantics=sem, vmem_limit_bytes=VMEM_LIMIT)


def _sigmoid(x):
    return 0.5 * jnp.tanh(0.5 * x) + 0.5


def _gelu_tanh(x):
    return 0.5 * x * (1.0 + jnp.tanh(math.sqrt(2.0 / math.pi) * (x + 0.044715 * (x * x * x))))


def _row_mod(mod_ref, b, ctx_row, row0, nrows, n_lat):
    mb = mod_ref[pl.ds(b, 1), :]
    mc = mod_ref[ctx_row:ctx_row + 1, :]
    rows = row0 + lax.broadcasted_iota(I32, (nrows, 1), 0)
    return jnp.where(rows < n_lat, mb, mc)


def _ln_mod(x, g, shift, scale):
    ms = jnp.mean(x * x, axis=-1, keepdims=True)
    y = (x * lax.rsqrt(ms + EPS)) * g
    return y * (1.0 + scale) + shift


def _mod_kernel(c_ref, w_ref, b_ref, o_ref):
    c = c_ref[...]
    s = c * _sigmoid(c)
    o_ref[...] = jnp.dot(s, w_ref[...], preferred_element_type=F32,
                         precision=lax.Precision.HIGHEST) + b_ref[...]


def _mod_tables(cc, ada_w, ada_b):
    depth, d, n6 = ada_w.shape
    tn = _pick(n6, (1536, 768, 384))
    return pl.pallas_call(
        _mod_kernel,
        grid=(depth, n6 // tn),
        in_specs=[pl.BlockSpec((MOD_ROWS, d), lambda i, j: (0, 0)),
                  pl.BlockSpec((None, d, tn), lambda i, j: (i, 0, j)),
                  pl.BlockSpec((None, 1, tn), lambda i, j: (i, 0, j))],
        out_specs=pl.BlockSpec((None, MOD_ROWS, tn), lambda i, j: (i, 0, j)),
        out_shape=jax.ShapeDtypeStruct((depth, MOD_ROWS, n6), F32),
        compiler_params=_cparams(("arbitrary", "arbitrary")),
        name="mod_tables",
    )(cc, ada_w, ada_b.reshape(depth, 1, n6))


def _qkv_kernel(h_ref, g_ref, sh_ref, sc_ref, wqk_ref, wvt_ref, cos_ref, s1_ref, s2_ref,
                qk_ref, vt_ref, *, n_lat, ctx_row):
    b = pl.program_id(0)
    i = pl.program_id(1)
    tt, d = h_ref.shape
    shift = _row_mod(sh_ref, b, ctx_row, i * tt, tt, n_lat)
    scale = _row_mod(sc_ref, b, ctx_row, i * tt, tt, n_lat)
    xn = _ln_mod(h_ref[...], g_ref[...], shift, scale).astype(BF16)
    cw = 2 * HEAD_W
    cos = jnp.concatenate([cos_ref[...]] * 2, axis=1)
    s1 = jnp.concatenate([s1_ref[...]] * 2, axis=1)
    s2 = jnp.concatenate([s2_ref[...]] * 2, axis=1)
    for n in range(2 * d // cw):
        acc = jnp.dot(xn, wqk_ref[:, n * cw:(n + 1) * cw], preferred_element_type=F32)
        r = acc * cos + pltpu.roll(acc, cw - 16, 1) * s1 + pltpu.roll(acc, 16, 1) * s2
        if n * cw < d:
            r = r * ((0.5 * HEAD_W) ** -0.5 * math.log2(math.e))
        qk_ref[:, n * cw:(n + 1) * cw] = r.astype(BF16)
    for n in range(d // cw):
        vt = lax.dot_general(wvt_ref[n * cw:(n + 1) * cw, :], xn, (((1,), (1,)), ((), ())),
                             preferred_element_type=F32).astype(BF16)
        gk = vt_ref.shape[2]
        for s in range(vt_ref.shape[0]):
            vt_ref[s, n * cw:(n + 1) * cw, :] = vt[:, s * gk:(s + 1) * gk]


def _qkv(h, g, mod, wqk, wvt, cos, s1, s2, *, n_lat, tt):
    bsz, t, d = h.shape
    vs, gk = 1, tt
    kern = functools.partial(_qkv_kernel, n_lat=n_lat, ctx_row=bsz)
    return pl.pallas_call(
        kern,
        grid=(bsz, t // tt),
        in_specs=[pl.BlockSpec((None, tt, d), lambda b, i: (b, i, 0)),
                  pl.BlockSpec((1, d), lambda b, i: (0, 0)),
                  pl.BlockSpec((MOD_ROWS, d), lambda b, i: (0, 0)),
                  pl.BlockSpec((MOD_ROWS, d), lambda b, i: (0, 1)),
                  pl.BlockSpec((d, 2 * d), lambda b, i: (0, 0)),
                  pl.BlockSpec((d, d), lambda b, i: (0, 0)),
                  pl.BlockSpec((tt, HEAD_W), lambda b, i: (i, 0)),
                  pl.BlockSpec((tt, HEAD_W), lambda b, i: (i, 0)),
                  pl.BlockSpec((tt, HEAD_W), lambda b, i: (i, 0))],
        out_specs=[pl.BlockSpec((None, tt, 2 * d), lambda b, i: (b, i, 0)),
                   pl.BlockSpec((None, vs, d, gk), lambda b, i: (b, i, 0, 0))],
        out_shape=[jax.ShapeDtypeStruct((bsz, t, 2 * d), BF16),
                   jax.ShapeDtypeStruct((bsz, t // gk, d, gk), BF16)],
        compiler_params=_cparams(("arbitrary", "arbitrary")),
        name="qkv_rope",
    )(h, g, mod, mod, wqk, wvt, cos, s1, s2)


def _rope_tables(n_lat, n_ctx):
    freqs = HEAD_W // 8
    rows = n_lat // GRID_W
    row = jnp.repeat(jnp.arange(rows), GRID_W).astype(F32)
    col = jnp.tile(jnp.arange(GRID_W), rows).astype(F32)
    inv = ROPE_THETA ** (-(jnp.arange(freqs, dtype=F32) * 2.0) / (2 * freqs))
    lane = jnp.arange(HEAD_W)
    dd = lane % (HEAD_W // 2)
    axis = dd // (2 * freqs)
    half = (dd % (2 * freqs)) // freqs
    f = dd % freqs
    pos = jnp.where(axis[None, :] == 0, row[:, None], col[:, None])
    ang = pos * inv[f][None, :]
    cos = jnp.cos(ang)
    sin = jnp.sin(ang)
    s1 = jnp.where(half[None, :] == 0, -sin, 0.0)
    s2 = jnp.where(half[None, :] == 1, sin, 0.0)
    pad = lambda a, v: jnp.concatenate([a, jnp.full((n_ctx, HEAD_W), v, F32)], axis=0)
    return pad(cos, 1.0), pad(s1, 0.0), pad(s2, 0.0)


def _diff_lambda(lam_ref, lam_init):
    lv = lam_ref[...]
    return (jnp.exp(jnp.sum(lv[0:1, :] * lv[1:2, :], axis=1, keepdims=True))
            - jnp.exp(jnp.sum(lv[2:3, :] * lv[3:4, :], axis=1, keepdims=True)) + lam_init)


def _split_q(q_ref, hh):
    q = q_ref[:, hh * HEAD_W:(hh + 1) * HEAD_W].astype(F32)
    lane = lax.broadcasted_iota(I32, q.shape, 1)
    return jnp.concatenate([jnp.where(lane < HEAD_W // 2, q, 0.0),
                            jnp.where(lane >= HEAD_W // 2, q, 0.0)], axis=0).astype(BF16)


def _attn_finish(o_ref, sg_ref, hh, l8, acc, lam, lam_init):
    tq = o_ref.shape[0]
    r = 1.0 / jnp.sum(l8, axis=0, keepdims=True)
    ot = acc[:, :tq] * r[:, :tq] - lam * (acc[:, tq:] * r[:, tq:])
    msq = jnp.mean(ot * ot, axis=0, keepdims=True)
    ot = (ot * lax.rsqrt(msq + EPS)) * sg_ref[...] * (1.0 - lam_init)
    o_ref[:, hh * HEAD_W:(hh + 1) * HEAD_W] = ot.T.astype(BF16)


def _attn_pipe_kernel(lam_ref, sg_ref, q_ref, k_ref, vt_ref, o_ref, s0_scr, s1_scr, m0_scr, m1_scr,
                      acc_scr, *, lam_init):
    f = pl.program_id(1)
    tq = q_ref.shape[0]
    nh = q_ref.shape[1] // HEAD_W
    ng, _, gk = vt_ref.shape

    @pl.when(f == 0)
    def _():
        s1_scr[...] = jnp.zeros_like(s1_scr)
        m1_scr[...] = jnp.zeros_like(m1_scr)

    def step(sw_scr, mw_scr, sr_scr, mr_scr):
        lam = _diff_lambda(lam_ref, lam_init)
        qbd = [_split_q(q_ref, hh) for hh in range(nh)]
        m_prev = [mr_scr[hh] for hh in range(nh)]
        acc_scr[...] = jnp.zeros_like(acc_scr)

        def body(g, carry):
            m8s, l8s = carry
            base = pl.multiple_of(g * gk, gk)
            new_m8, new_l8 = [], []
            for hh in range(nh):
                cols = slice(hh * HEAD_W, (hh + 1) * HEAD_W)
                s = lax.dot_general(k_ref[pl.ds(base, gk), cols], qbd[hh],
                                    (((1,), (1,)), ((), ())), preferred_element_type=F32)
                sw_scr[hh, pl.ds(base, gk), :] = s
                new_m8.append(jnp.maximum(
                    m8s[hh], jnp.max(s.reshape(gk // SUBLANES, SUBLANES, 2 * tq), axis=0)))
                e = jnp.exp2(sr_scr[hh, pl.ds(base, gk), :] - m_prev[hh])
                new_l8.append(
                    l8s[hh] + jnp.sum(e.reshape(gk // SUBLANES, SUBLANES, 2 * tq), axis=0))
                acc_scr[hh] += jnp.dot(vt_ref[g, cols, :], e.astype(BF16),
                                       preferred_element_type=F32)
            return tuple(new_m8), tuple(new_l8)

        init = (tuple(jnp.full((SUBLANES, 2 * tq), -jnp.inf, F32) for _ in range(nh)),
                tuple(jnp.zeros((SUBLANES, 2 * tq), F32) for _ in range(nh)))
        m8s, l8s = init
        for g in range(ng):
            m8s, l8s = body(g, (m8s, l8s))
        for hh in range(nh):
            mw_scr[hh] = jnp.max(m8s[hh], axis=0, keepdims=True)
            _attn_finish(o_ref, sg_ref, hh, l8s[hh], acc_scr[hh], lam, lam_init)

    @pl.when(lax.rem(f, 2) == 0)
    def _():
        step(s0_scr, m0_scr, s1_scr, m1_scr)

    @pl.when(lax.rem(f, 2) == 1)
    def _():
        step(s1_scr, m1_scr, s0_scr, m0_scr)


def _attention_lat(qk, vt, lamv, sg, *, n_lat, tq, nh, lam_init):
    bsz, t, d2 = qk.shape
    d = d2 // 2
    hw = nh * HEAD_W
    heads = d // hw
    ng, gk = vt.shape[1], vt.shape[3]
    n_tiles = n_lat // tq
    last = heads * n_tiles - 1
    cur = lambda f: jnp.minimum(f, last)
    prev = lambda f: jnp.maximum(f - 1, 0)
    kern = functools.partial(_attn_pipe_kernel, lam_init=lam_init)
    return pl.pallas_call(
        kern,
        grid=(bsz, heads * n_tiles + 1),
        in_specs=[pl.BlockSpec((4, HEAD_W // 2), lambda b, f: (0, 0)),
                  pl.BlockSpec((HEAD_W, 1), lambda b, f: (0, 0)),
                  pl.BlockSpec((None, tq, hw), lambda b, f: (b, cur(f) % n_tiles, cur(f) // n_tiles)),
                  pl.BlockSpec((None, t, hw), lambda b, f: (b, 0, heads + cur(f) // n_tiles)),
                  pl.BlockSpec((None, ng, hw, gk), lambda b, f: (b, 0, prev(f) // n_tiles, 0))],
        out_specs=pl.BlockSpec((None, tq, hw),
                               lambda b, f: (b, prev(f) % n_tiles, prev(f) // n_tiles)),
        out_shape=jax.ShapeDtypeStruct((bsz, n_lat, d), BF16),
        scratch_shapes=[pltpu.VMEM((nh, t, 2 * tq), F32), pltpu.VMEM((nh, t, 2 * tq), F32),
                        pltpu.VMEM((nh, 1, 2 * tq), F32), pltpu.VMEM((nh, 1, 2 * tq), F32),
                        pltpu.VMEM((nh, HEAD_W, 2 * tq), F32)],
        compiler_params=_cparams(("arbitrary", "arbitrary")),
        name="diff_attn_lat",
    )(lamv, sg, qk, qk, vt)


def _attn_kernel(lam_ref, sg_ref, q_ref, k_ref, vt_ref, o_ref, s_scr, *, kc, v_off, lam_init):
    tq = q_ref.shape[0]
    tk = k_ref.shape[0]
    nh = q_ref.shape[1] // HEAD_W
    lam = _diff_lambda(lam_ref, lam_init)

    chunks = range(tk // kc)
    ms = []
    for hh in range(nh):
        qb = _split_q(q_ref, hh)
        m8 = jnp.full((SUBLANES, 2 * tq), -jnp.inf, F32)
        for c in chunks:
            s = lax.dot_general(k_ref[c * kc:(c + 1) * kc, hh * HEAD_W:(hh + 1) * HEAD_W], qb,
                                (((1,), (1,)), ((), ())), preferred_element_type=F32)
            s_scr[hh, c * kc:(c + 1) * kc, :] = s
            m8 = jnp.maximum(m8, jnp.max(s.reshape(kc // SUBLANES, SUBLANES, 2 * tq), axis=0))
        ms.append(jnp.max(m8, axis=0, keepdims=True))
    for hh in range(nh):
        l8 = jnp.zeros((SUBLANES, 2 * tq), F32)
        acc = jnp.zeros((HEAD_W, 2 * tq), F32)
        for c in chunks:
            e = jnp.exp2(s_scr[hh, c * kc:(c + 1) * kc, :] - ms[hh])
            l8 = l8 + jnp.sum(e.reshape(kc // SUBLANES, SUBLANES, 2 * tq), axis=0)
            acc = acc + jnp.dot(
                vt_ref[hh * HEAD_W:(hh + 1) * HEAD_W, v_off + c * kc:v_off + (c + 1) * kc],
                e.astype(BF16), preferred_element_type=F32)
        _attn_finish(o_ref, sg_ref, hh, l8, acc, lam, lam_init)


def _attention_ctx(qk, vt, lamv, sg, *, n_lat, nh, lam_init):
    bsz, t, d2 = qk.shape
    d = d2 // 2
    hw = nh * HEAD_W
    heads = d // hw
    n_ctx = t - n_lat
    gk = vt.shape[3]
    assert n_lat % n_ctx == 0 and n_lat % gk + n_ctx <= gk
    rb = n_lat // n_ctx
    kern = functools.partial(_attn_kernel, kc=n_ctx, v_off=n_lat % gk, lam_init=lam_init)
    return pl.pallas_call(
        kern,
        grid=(bsz, heads),
        in_specs=[pl.BlockSpec((4, HEAD_W // 2), lambda b, h: (0, 0)),
                  pl.BlockSpec((HEAD_W, 1), lambda b, h: (0, 0)),
                  pl.BlockSpec((None, n_ctx, hw), lambda b, h: (b, rb, h)),
                  pl.BlockSpec((None, n_ctx, hw), lambda b, h: (b, rb, heads + h)),
                  pl.BlockSpec((None, None, hw, gk), lambda b, h: (b, n_lat // gk, h, 0))],
        out_specs=pl.BlockSpec((None, n_ctx, hw), lambda b, h: (b, 0, h)),
        out_shape=jax.ShapeDtypeStruct((bsz, n_ctx, d), BF16),
        scratch_shapes=[pltpu.VMEM((nh, n_ctx, 2 * n_ctx), F32)],
        compiler_params=_cparams(("arbitrary", "arbitrary")),
        name="diff_attn_ctx",
    )(lamv, sg, qk, qk, vt)


def _router_probs(m, wr, n_exp):
    m_hi = m.astype(BF16)
    m_lo = (m - m_hi.astype(F32)).astype(BF16)
    w_hi = wr.astype(BF16)
    w_lo = (wr - w_hi.astype(F32)).astype(BF16)
    logits = (jnp.dot(m_hi, w_hi, preferred_element_type=F32)
              + jnp.dot(m_lo, w_hi, preferred_element_type=F32)
              + jnp.dot(m_hi, w_lo, preferred_element_type=F32))
    lane = lax.broadcasted_iota(I32, logits.shape, 1)
    lg = jnp.where(lane < n_exp, logits, -jnp.inf)
    ex = jnp.exp(lg - jnp.max(lg, axis=1, keepdims=True))
    return m_hi, ex / jnp.sum(ex, axis=1, keepdims=True)


def _proj_res_kernel(u_ref, w_ref, h_ref, gate_ref, o_ref, *, n_lat, ctx_row, nc):
    b = pl.program_id(0)
    i = pl.program_id(1)
    tt, d = h_ref.shape
    gate = _row_mod(gate_ref, b, ctx_row, i * tt, tt, n_lat)
    u = u_ref[...]
    for n in range(d // nc):
        y = jnp.dot(u, w_ref[:, n * nc:(n + 1) * nc], preferred_element_type=F32)
        o_ref[:, n * nc:(n + 1) * nc] = (h_ref[:, n * nc:(n + 1) * nc]
                                         + gate[:, n * nc:(n + 1) * nc] * y)


def _proj_res(u, w, h, mod, *, gate_chunk, n_rows, n_lat, tt):
    bsz, _, k = u.shape
    d = h.shape[-1]
    kern = functools.partial(_proj_res_kernel, n_lat=n_lat, ctx_row=bsz, nc=256)
    return pl.pallas_call(
        kern,
        grid=(bsz, n_rows // tt),
        in_specs=[pl.BlockSpec((None, tt, k), lambda b, i: (b, i, 0)),
                  pl.BlockSpec((k, d), lambda b, i: (0, 0)),
                  pl.BlockSpec((None, tt, d), lambda b, i: (b, i, 0)),
                  pl.BlockSpec((MOD_ROWS, d), lambda b, i: (0, gate_chunk))],
        out_specs=pl.BlockSpec((None, tt, d), lambda b, i: (b, i, 0)),
        out_shape=jax.ShapeDtypeStruct((bsz, n_rows, d), F32),
        compiler_params=_cparams(("arbitrary", "arbitrary")),
        name="proj_residual",
    )(u, w, h, mod)


def _norm_probs_kernel(h_ref, g_ref, sh_ref, sc_ref, wr_ref, m_ref, p_ref, *,
                       n_lat, ctx_row, n_exp):
    b = pl.program_id(0)
    i = pl.program_id(1)
    tt, d = h_ref.shape
    shift = _row_mod(sh_ref, b, ctx_row, i * tt, tt, n_lat)
    scale = _row_mod(sc_ref, b, ctx_row, i * tt, tt, n_lat)
    m = _ln_mod(h_ref[...], g_ref[...], shift, scale)
    m_ref[...], p_ref[...] = _router_probs(m, wr_ref[...], n_exp)


def _norm_probs(h, g, mod, wr, *, n_lat, n_exp, tt):
    bsz, t, d = h.shape
    kern = functools.partial(_norm_probs_kernel, n_lat=n_lat, ctx_row=bsz, n_exp=n_exp)
    return pl.pallas_call(
        kern,
        grid=(bsz, t // tt),
        in_specs=[pl.BlockSpec((None, tt, d), lambda b, i: (b, i, 0)),
                  pl.BlockSpec((1, d), lambda b, i: (0, 0)),
                  pl.BlockSpec((MOD_ROWS, d), lambda b, i: (0, 3)),
                  pl.BlockSpec((MOD_ROWS, d), lambda b, i: (0, 4)),
                  pl.BlockSpec((d, LANES), lambda b, i: (0, 0))],
        out_specs=[pl.BlockSpec((None, tt, d), lambda b, i: (b, i, 0)),
                   pl.BlockSpec((None, tt, LANES), lambda b, i: (b, i, 0))],
        out_shape=[jax.ShapeDtypeStruct((bsz, t, d), BF16),
                   jax.ShapeDtypeStruct((bsz, t, LANES), F32)],
        compiler_params=_cparams(("arbitrary", "arbitrary")),
        name="moe_norm_probs",
    )(h, g, mod, mod, wr)


def _lru_in_kernel(h_ref, g_ref, sh_ref, sc_ref, w_ref, bias_ref, gy_ref, xr_ref, *,
                   n_lat, ctx_row, nc):
    b = pl.program_id(0)
    i = pl.program_id(1)
    tt, d = h_ref.shape
    shift = _row_mod(sh_ref, b, ctx_row, i * tt, tt, n_lat)
    scale = _row_mod(sc_ref, b, ctx_row, i * tt, tt, n_lat)
    xn = _ln_mod(h_ref[...], g_ref[...], shift, scale).astype(BF16)
    for n in range(d // nc):
        y = jnp.dot(xn, w_ref[:, n * nc:(n + 1) * nc], preferred_element_type=F32)
        y = y + bias_ref[:, n * nc:(n + 1) * nc]
        gy_ref[:, n * nc:(n + 1) * nc] = _gelu_tanh(y).astype(BF16)
    for n in range(d // nc):
        x = jnp.dot(xn, w_ref[:, d + n * nc:d + (n + 1) * nc], preferred_element_type=F32)
        xr_ref[:, n * nc:(n + 1) * nc] = x + bias_ref[:, d + n * nc:d + (n + 1) * nc]


def _lru_in(h, g, mod, w_in, b_in, *, n_lat, tt):
    bsz, t, d = h.shape
    kern = functools.partial(_lru_in_kernel, n_lat=n_lat, ctx_row=bsz, nc=256)
    return pl.pallas_call(
        kern,
        grid=(bsz, t // tt),
        in_specs=[pl.BlockSpec((None, tt, d), lambda b, i: (b, i, 0)),
                  pl.BlockSpec((1, d), lambda b, i: (0, 0)),
                  pl.BlockSpec((MOD_ROWS, d), lambda b, i: (0, 0)),
                  pl.BlockSpec((MOD_ROWS, d), lambda b, i: (0, 1)),
                  pl.BlockSpec((d, 2 * d), lambda b, i: (0, 0)),
                  pl.BlockSpec((1, 2 * d), lambda b, i: (0, 0))],
        out_specs=[pl.BlockSpec((None, tt, d), lambda b, i: (b, i, 0)),
                   pl.BlockSpec((None, tt, d), lambda b, i: (b, i, 0))],
        out_shape=[jax.ShapeDtypeStruct((bsz, t, d), BF16),
                   jax.ShapeDtypeStruct((bsz, t, d), F32)],
        compiler_params=_cparams(("arbitrary", "arbitrary")),
        name="lru_in_proj",
    )(h, g, mod, mod, w_in, b_in)


def _lru_kernel(xr_ref, gy_ref, cw_ref, cb_ref, wg_ref, bg_ref, lam_ref, u_ref,
                xp_scr, hf_scr, hb_scr, *, n_lat):
    t, cw = xr_ref.shape
    n_ctx = t - n_lat
    seg, pitch, nsub = LRU_SEG, LRU_PITCH, SUBLANES
    rc = seg * nsub
    nseg_lat, nseg_ctx = n_lat // seg, n_ctx // seg
    lat0, ctx0 = 1, nseg_lat + 2
    zseg = jnp.zeros((seg, cw), F32)
    for s in (0, nseg_lat + 1, nseg_lat + nseg_ctx + 2):
        xp_scr[s * pitch:s * pitch + seg, :] = zseg
    for s in range(nseg_lat):
        xp_scr[(lat0 + s) * pitch:(lat0 + s) * pitch + seg, :] = xr_ref[s * seg:(s + 1) * seg, :]
    for s in range(nseg_ctx):
        xp_scr[(ctx0 + s) * pitch:(ctx0 + s) * pitch + seg, :] = (
            xr_ref[n_lat + s * seg:n_lat + (s + 1) * seg, :])
    lam = lam_ref[...]
    sp = jnp.maximum(-lam, 0.0) + jnp.log(1.0 + jnp.exp(-jnp.abs(lam)))
    w = cw_ref[...]
    cb = cb_ref[...]

    def rows8(ref, start):
        return ref[pl.ds(start, nsub, stride=pitch), :]

    def chunk(pb, carry, d, reverse, out_scr):
        x = [rows8(xp_scr, pb + g) for g in range(seg)]
        xm1 = [rows8(xp_scr, pb - (pitch - seg) - 1)] + x[:-1]
        xm2 = [rows8(xp_scr, pb - (pitch - seg) - 2), xm1[0]] + x[:-2]
        xp1 = x[1:] + [rows8(xp_scr, pb + pitch)]
        xc = jnp.concatenate(
            [cb + w[0:1, :] * xm2[g] + w[1:2, :] * xm1[g] + w[2:3, :] * x[g] + w[3:4, :] * xp1[g]
             for g in range(seg)], axis=0)
        gpre = jnp.dot(xc.astype(BF16), wg_ref[d, 0], preferred_element_type=F32) + bg_ref[d, 0]
        r = _sigmoid(gpre[:, :LRU_BW])
        ig = _sigmoid(gpre[:, LRU_BW:])
        a = jnp.exp((-LRU_C * sp[d:d + 1, :]) * r)
        bt = jnp.sqrt(1.0 - a * a) * (ig * xc)
        a3 = a.reshape(seg, nsub, cw)
        b3 = bt.reshape(seg, nsub, cw)
        h = jnp.zeros((nsub, cw), F32)
        p = jnp.ones((nsub, cw), F32)
        hl, pl_ = [None] * seg, [None] * seg
        for g in (range(seg - 1, -1, -1) if reverse else range(seg)):
            h = a3[g] * h + b3[g]
            p = a3[g] * p
            hl[g], pl_[g] = h, p
        entry = [None] * nsub
        for j in (range(nsub - 1, -1, -1) if reverse else range(nsub)):
            entry[j] = carry
            carry = h[j:j + 1, :] + p[j:j + 1, :] * carry
        h_in = jnp.concatenate(entry, axis=0)
        for g in range(seg):
            out_scr[pl.ds(pb + g, nsub, stride=pitch), :] = hl[g] + pl_[g] * h_in
        return carry

    lat_pb = lambda c: lat0 * pitch + c * (nsub * pitch)
    ctx_pb = lambda c: (ctx0 + c * nsub) * pitch
    n_lat_chunks, n_ctx_chunks = n_lat // rc, n_ctx // rc
    cf = cb_ = jnp.zeros((1, cw), F32)
    for c in range(n_ctx_chunks):
        cf = chunk(ctx_pb(c), cf, 0, False, hf_scr)
        cb_ = chunk(ctx_pb(n_ctx_chunks - 1 - c), cb_, 1, True, hb_scr)

    def both(c, carries):
        return (chunk(lat_pb(c), carries[0], 0, False, hf_scr),
                chunk(lat_pb(n_lat_chunks - 1 - c), carries[1], 1, True, hb_scr))

    lax.fori_loop(0, n_lat_chunks, both, (cf, cb_))
    for s in range(nseg_lat):
        rows = slice((lat0 + s) * pitch, (lat0 + s) * pitch + seg)
        u_ref[s * seg:(s + 1) * seg, :] = (gy_ref[s * seg:(s + 1) * seg, :].astype(F32)
                                           * (hf_scr[rows, :] + hb_scr[rows, :])).astype(BF16)


def _lru_core(xr, gy, conv_w, conv_b, w_gates, b_gates, lam, *, n_lat):
    bsz, t, d = xr.shape
    cw = LRU_BW
    rows = ((t // LRU_SEG) + 3) * LRU_PITCH
    kern = functools.partial(_lru_kernel, n_lat=n_lat)
    return pl.pallas_call(
        kern,
        grid=(bsz, d // cw),
        in_specs=[pl.BlockSpec((None, t, cw), lambda b, k: (b, 0, k)),
                  pl.BlockSpec((None, n_lat, cw), lambda b, k: (b, 0, k)),
                  pl.BlockSpec((CONV_W, cw), lambda b, k: (0, k)),
                  pl.BlockSpec((1, cw), lambda b, k: (0, k)),
                  pl.BlockSpec((2, 1, LRU_BW, 2 * LRU_BW), lambda b, k: (0, k, 0, 0)),
                  pl.BlockSpec((2, 1, 1, 2 * LRU_BW), lambda b, k: (0, k, 0, 0)),
                  pl.BlockSpec((2, cw), lambda b, k: (0, k))],
        out_specs=pl.BlockSpec((None, n_lat, cw), lambda b, k: (b, 0, k)),
        out_shape=jax.ShapeDtypeStruct((bsz, n_lat, d), BF16),
        scratch_shapes=[pltpu.VMEM((rows, cw), F32)] * 3,
        compiler_params=_cparams(("arbitrary", "arbitrary")),
        name="lru_core",
    )(xr, gy, conv_w, conv_b, w_gates, b_gates, lam)


def _select(p, cap, n_exp):
    n_tok = p.shape[0]
    n_rows = -(-n_exp // SUBLANES) * SUBLANES
    pt = p.T[0:n_rows, :]

    def body(i, thr):
        cand = thr | jnp.left_shift(jnp.int32(1), 29 - i)
        hit = jnp.where(pt >= lax.bitcast_convert_type(cand, F32), 1.0, 0.0)
        return jnp.where(jnp.sum(hit, axis=1, keepdims=True) >= cap, cand, thr)

    thr_col = lax.fori_loop(0, 30, body, jnp.zeros((n_rows, 1), I32))
    diag = (lax.broadcasted_iota(I32, (n_rows, LANES), 0)
            == lax.broadcasted_iota(I32, (n_rows, LANES), 1))
    thr = jnp.sum(jnp.where(diag, jnp.broadcast_to(thr_col, (n_rows, LANES)), 0),
                  axis=0, keepdims=True)
    gt = p >= pltpu.bitcast(thr + 1, F32)
    eq = (p >= pltpu.bitcast(thr, F32)) & jnp.logical_not(gt)
    blk = PREFIX_BLOCK
    ltri = jnp.where(lax.broadcasted_iota(I32, (blk, blk), 1)
                     < lax.broadcasted_iota(I32, (blk, blk), 0), 1.0, 0.0).astype(BF16)
    masks = jnp.concatenate([jnp.where(gt, 1.0, 0.0), jnp.where(eq, 1.0, 0.0)], axis=1)
    off = jnp.zeros((1, 2 * LANES), F32)
    pres = []
    for i in range(n_tok // blk):
        mb = masks[i * blk:(i + 1) * blk, :]
        pre = jnp.dot(ltri, mb.astype(BF16), preferred_element_type=F32) + off
        pres.append(pre)
        off = pre[blk - 1:blk, :] + mb[blk - 1:blk, :]
    pre = jnp.concatenate(pres, axis=0)
    pre_gt, pre_eq = pre[:, :LANES], pre[:, LANES:]
    need = cap - off[:, :LANES]
    sel = gt | (eq & (pre_eq < need))
    slot = pre_gt + jnp.minimum(pre_eq, need)
    bounds = jnp.concatenate([slot[i * blk:i * blk + 1, :] for i in range(n_tok // blk)]
                             + [jnp.full((1, LANES), float(cap), F32)], axis=0)
    return jnp.where(sel, slot, -1.0), jnp.where(sel, p, 0.0), bounds


def _select_kernel(p_ref, slot_ref, gate_ref, slot_t_ref, gate_t_ref, bounds_ref, *,
                   n_lat, n_exp):
    t = p_ref.shape[0]
    n_ctx = t - n_lat
    slot, gate, bounds = _select(p_ref[0:n_lat, :], EC_CAPACITY * n_lat // n_exp, n_exp)
    slot_ref[0:n_lat, :] = slot
    gate_ref[0:n_lat, :] = gate
    slot_t_ref[:, 0:n_lat] = slot.T
    gate_t_ref[:, 0:n_lat] = gate.T
    bounds_ref[...] = jnp.zeros_like(bounds_ref)
    bounds_ref[0:bounds.shape[0], :] = bounds
    if n_ctx:
        slot, gate, _ = _select(p_ref[n_lat:t, :], EC_CAPACITY * n_ctx // n_exp, n_exp)
        slot_ref[n_lat:t, :] = slot
        gate_ref[n_lat:t, :] = gate
        slot_t_ref[:, n_lat:t] = slot.T
        gate_t_ref[:, n_lat:t] = gate.T


def _select_tokens(p, *, n_lat, n_exp):
    bsz, t, _ = p.shape
    nb_rows = -(-(n_lat // PREFIX_BLOCK + 1) // SUBLANES) * SUBLANES
    kern = functools.partial(_select_kernel, n_lat=n_lat, n_exp=n_exp)
    return pl.pallas_call(
        kern,
        grid=(bsz,),
        in_specs=[pl.BlockSpec((None, t, LANES), lambda b: (b, 0, 0))],
        out_specs=[pl.BlockSpec((None, t, LANES), lambda b: (b, 0, 0)),
                   pl.BlockSpec((None, t, LANES), lambda b: (b, 0, 0)),
                   pl.BlockSpec((None, LANES, t), lambda b: (b, 0, 0)),
                   pl.BlockSpec((None, LANES, t), lambda b: (b, 0, 0)),
                   pl.BlockSpec((None, nb_rows, LANES), lambda b: (b, 0, 0))],
        out_shape=[jax.ShapeDtypeStruct((bsz, t, LANES), F32),
                   jax.ShapeDtypeStruct((bsz, t, LANES), F32),
                   jax.ShapeDtypeStruct((bsz, LANES, t), F32),
                   jax.ShapeDtypeStruct((bsz, LANES, t), F32),
                   jax.ShapeDtypeStruct((bsz, nb_rows, LANES), F32)],
        compiler_params=_cparams(("arbitrary",)),
        name="moe_select",
    )(p)


def _gather_kernel(bounds_ref, m_ref, slot_t_ref, xg_ref, acc_scr, *, n_lat, cap_lat, cap_ctx):
    b = pl.program_id(0)
    kb = pl.program_id(1)
    n_exp = xg_ref.shape[0]
    blk, d = m_ref.shape
    n_lat_blocks = n_lat // blk

    @pl.when(kb == 0)
    def _():
        acc_scr[...] = jnp.zeros_like(acc_scr)

    base = (b * (n_lat_blocks + 1) + jnp.minimum(kb, n_lat_blocks - 1)) * n_exp
    los = [bounds_ref[base + e] for e in range(n_exp)]
    his = [bounds_ref[base + n_exp + e] for e in range(n_exp)]
    taken = kb >= n_lat_blocks

    def one_hot(rows, first):
        sl = lax.broadcasted_iota(I32, (rows, 1), 0).astype(F32)
        return [jnp.where(slot_t_ref[e:e + 1, :] - first[e] == sl, 1.0, 0.0).astype(BF16)
                for e in range(n_exp)]

    def window_path(win, starts):
        p = jnp.concatenate(one_hot(win, [s.astype(F32) for s in starts]), axis=0)
        res = jnp.dot(p, m_ref[...], preferred_element_type=F32)
        for e in range(n_exp):
            acc_scr[e, pl.ds(starts[e], win), :] += res[e * win:(e + 1) * win, :]

    for win in (64, 128):
        if win > cap_lat:
            continue
        starts, fits = [], None
        for e in range(n_exp):
            st = jnp.minimum(lax.shift_left(lax.shift_right_logical(los[e], 4), 4),
                             cap_lat - win)
            starts.append(pl.multiple_of(st, 16))
            ok = his[e] - st <= win
            fits = ok if fits is None else jnp.logical_and(fits, ok)
        pl.when(jnp.logical_and(jnp.logical_not(taken), fits))(
            functools.partial(window_path, win, starts))
        taken = jnp.logical_or(taken, fits)

    @pl.when(jnp.logical_not(taken))
    def _():
        p = jnp.concatenate(one_hot(cap_lat, [0.0] * n_exp), axis=0)
        res = jnp.dot(p, m_ref[...], preferred_element_type=F32)
        for e in range(n_exp):
            acc_scr[e] += res[e * cap_lat:(e + 1) * cap_lat, :]

    @pl.when(kb == n_lat_blocks - 1)
    def _():
        for e in range(n_exp):
            xg_ref[e, 0:cap_lat, :] = acc_scr[e].astype(BF16)

    if cap_ctx:
        @pl.when(kb >= n_lat_blocks)
        def _():
            p = jnp.concatenate(one_hot(cap_ctx, [0.0] * n_exp), axis=0)
            res = jnp.dot(p, m_ref[...], preferred_element_type=F32).astype(BF16)
            for e in range(n_exp):
                xg_ref[e, cap_lat:cap_lat + cap_ctx, :] = res[e * cap_ctx:(e + 1) * cap_ctx, :]


def _gather(m, slot_t, table, *, n_lat, n_exp):
    bsz, t, d = m.shape
    blk = PREFIX_BLOCK
    cap_lat = EC_CAPACITY * n_lat // n_exp
    cap_ctx = EC_CAPACITY * (t - n_lat) // n_exp
    assert t - n_lat in (0, blk)
    r = cap_lat + cap_ctx
    kern = functools.partial(_gather_kernel, n_lat=n_lat, cap_lat=cap_lat, cap_ctx=cap_ctx)
    grid_spec = pltpu.PrefetchScalarGridSpec(
        num_scalar_prefetch=1,
        grid=(bsz, t // blk),
        in_specs=[pl.BlockSpec((None, blk, d), lambda b, kb, tbl: (b, kb, 0)),
                  pl.BlockSpec((None, LANES, blk), lambda b, kb, tbl: (b, 0, kb))],
        out_specs=pl.BlockSpec((n_exp, None, r, d), lambda b, kb, tbl: (0, b, 0, 0)),
        scratch_shapes=[pltpu.VMEM((n_exp, cap_lat, d), F32)])
    return pl.pallas_call(
        kern,
        grid_spec=grid_spec,
        out_shape=jax.ShapeDtypeStruct((n_exp, bsz, r, d), BF16),
        compiler_params=_cparams(("arbitrary", "arbitrary")),
        name="moe_gather",
    )(table, m, slot_t)


def _ffn_kernel(x_ref, wg_ref, wu_ref, wd_ref, y_ref, acc_scr, *, rc):
    j = pl.program_id(1)
    nj = pl.num_programs(1)
    rows = x_ref.shape[0]

    @pl.when(j == 0)
    def _():
        acc_scr[...] = jnp.zeros_like(acc_scr)

    wg = wg_ref[...].astype(BF16)
    wu = wu_ref[...].astype(BF16)
    wd = wd_ref[...].astype(BF16)

    for c in range(rows // rc):
        x = x_ref[c * rc:(c + 1) * rc, :]
        g = jnp.dot(x, wg, preferred_element_type=F32)
        u = jnp.dot(x, wu, preferred_element_type=F32)
        hid = ((g * _sigmoid(g)) * u).astype(BF16)
        acc_scr[c * rc:(c + 1) * rc, :] += jnp.dot(hid, wd, preferred_element_type=F32)

    @pl.when(j == nj - 1)
    def _():
        y_ref[...] = acc_scr[...].astype(BF16)


def _ffn(xg, w_gate_up, w_down, *, layer, fc, rc):
    n_exp, rows, d = xg.shape
    f = w_down.shape[2]
    nj = f // fc
    kern = functools.partial(_ffn_kernel, rc=rc)
    return pl.pallas_call(
        kern,
        grid=(n_exp, nj),
        in_specs=[pl.BlockSpec((None, rows, d), lambda e, j: (e, 0, 0)),
                  pl.BlockSpec((None, None, d, fc), lambda e, j: (layer, e, 0, j)),
                  pl.BlockSpec((None, None, d, fc), lambda e, j: (layer, e, 0, nj + j)),
                  pl.BlockSpec((None, None, fc, d), lambda e, j: (layer, e, j, 0))],
        out_specs=pl.BlockSpec((None, rows, d), lambda e, j: (e, 0, 0)),
        out_shape=jax.ShapeDtypeStruct((n_exp, rows, d), BF16),
        scratch_shapes=[pltpu.VMEM((rows, d), F32)],
        compiler_params=_cparams(("arbitrary", "arbitrary")),
        name="moe_ffn",
    )(xg, w_gate_up, w_gate_up, w_down)


def _combine_kernel(bounds_ref, y_ref, slot_ref, gate_ref, slot_t_ref, gate_t_ref, h_ref, g2_ref,
                    fg_ref, o_ref, *, n_lat, ctx_row, cap_lat, cap_ctx, final_norm):
    b = pl.program_id(0)
    i = pl.program_id(1)
    tt, d = h_ref.shape
    n_exp = y_ref.shape[0]
    n_lat_tiles = n_lat // tt

    def finish(acc):
        gate2 = _row_mod(g2_ref, b, ctx_row, i * tt, tt, n_lat)
        out = h_ref[...] + gate2 * acc
        if final_norm:
            ms = jnp.mean(out * out, axis=-1, keepdims=True)
            out = (out * lax.rsqrt(ms + EPS)) * fg_ref[...]
        o_ref[...] = out

    base = (b * (n_lat_tiles + 1) + jnp.minimum(i, n_lat_tiles - 1)) * n_exp
    los = [bounds_ref[base + e] for e in range(n_exp)]
    his = [bounds_ref[base + n_exp + e] for e in range(n_exp)]
    taken = i >= n_lat_tiles

    def window_path(win, starts):
        sl = lax.broadcasted_iota(I32, (win, 1), 0).astype(F32)
        ps, ys = [], []
        for e in range(n_exp):
            rel = slot_t_ref[e:e + 1, :] - starts[e].astype(F32)
            ps.append(jnp.where(rel == sl, gate_t_ref[e:e + 1, :], 0.0).astype(BF16))
            ys.append(y_ref[e, pl.ds(starts[e], win), :])
        finish(lax.dot_general(jnp.concatenate(ps, axis=0), jnp.concatenate(ys, axis=0),
                               (((0,), (0,)), ((), ())), preferred_element_type=F32))

    for win in (64, 128):
        if win > cap_lat:
            continue
        starts, fits = [], None
        for e in range(n_exp):
            st = jnp.minimum(lax.shift_left(lax.shift_right_logical(los[e], 4), 4),
                             cap_lat - win)
            starts.append(pl.multiple_of(st, 16))
            ok = his[e] - st <= win
            fits = ok if fits is None else jnp.logical_and(fits, ok)
        pl.when(jnp.logical_and(jnp.logical_not(taken), fits))(
            functools.partial(window_path, win, starts))
        taken = jnp.logical_or(taken, fits)

    @pl.when(jnp.logical_not(taken))
    def _():
        slot = slot_ref[...]
        gate = gate_ref[...]
        iota_c = lax.broadcasted_iota(I32, (1, cap_lat), 1).astype(F32)
        acc = jnp.zeros((tt, d), F32)
        for e in range(n_exp):
            pt = jnp.where(slot[:, e:e + 1] == iota_c, gate[:, e:e + 1], 0.0).astype(BF16)
            acc = acc + jnp.dot(pt, y_ref[e, 0:cap_lat, :], preferred_element_type=F32)
        finish(acc)

    if cap_ctx:
        @pl.when(i >= n_lat_tiles)
        def _():
            slot = slot_ref[...]
            gate = gate_ref[...]
            iota_c = lax.broadcasted_iota(I32, (1, n_exp * cap_ctx), 1).astype(F32)
            pt = jnp.zeros((tt, n_exp * cap_ctx), F32)
            for e in range(n_exp):
                pt = pt + jnp.where(slot[:, e:e + 1] + float(e * cap_ctx) == iota_c,
                                    gate[:, e:e + 1], 0.0)
            yc = y_ref[:, cap_lat:cap_lat + cap_ctx, :].reshape(n_exp * cap_ctx, d)
            finish(jnp.dot(pt.astype(BF16), yc, preferred_element_type=F32))


def _combine(y, slot, gate, slot_t, gate_t, table, h, mod, final_g, *, n_lat, n_out, final_norm):
    n_exp, bsz, r, d = y.shape
    t = slot.shape[1]
    tt = PREFIX_BLOCK
    cap_lat = EC_CAPACITY * n_lat // n_exp
    cap_ctx = EC_CAPACITY * (t - n_lat) // n_exp
    kern = functools.partial(_combine_kernel, n_lat=n_lat, ctx_row=bsz, cap_lat=cap_lat,
                             cap_ctx=cap_ctx, final_norm=final_norm)
    grid_spec = pltpu.PrefetchScalarGridSpec(
        num_scalar_prefetch=1,
        grid=(bsz, n_out // tt),
        in_specs=[pl.BlockSpec((n_exp, None, r, d), lambda b, i, tbl: (0, b, 0, 0)),
                  pl.BlockSpec((None, tt, LANES), lambda b, i, tbl: (b, i, 0)),
                  pl.BlockSpec((None, tt, LANES), lambda b, i, tbl: (b, i, 0)),
                  pl.BlockSpec((None, LANES, tt), lambda b, i, tbl: (b, 0, i)),
                  pl.BlockSpec((None, LANES, tt), lambda b, i, tbl: (b, 0, i)),
                  pl.BlockSpec((None, tt, d), lambda b, i, tbl: (b, i, 0)),
                  pl.BlockSpec((MOD_ROWS, d), lambda b, i, tbl: (0, 5)),
                  pl.BlockSpec((1, d), lambda b, i, tbl: (0, 0))],
        out_specs=pl.BlockSpec((None, tt, d), lambda b, i, tbl: (b, i, 0)))
    return pl.pallas_call(
        kern,
        grid_spec=grid_spec,
        out_shape=jax.ShapeDtypeStruct((bsz, n_out, d), F32),
        compiler_params=_cparams(("arbitrary", "arbitrary")),
        name="moe_combine",
    )(table, y, slot, gate, slot_t, gate_t, h, mod, final_g)


def _moe(h, m, p, mod, w_gate_up, w_down, final_g, *, layer, n_lat, n_exp, final_norm):
    bsz, t, d = h.shape
    slot, gate, slot_t, gate_t, bounds = _select_tokens(p, n_lat=n_lat, n_exp=n_exp)
    table = bounds[:, :n_lat // PREFIX_BLOCK + 1, :n_exp].astype(I32).reshape(-1)
    xg = _gather(m, slot_t, table, n_lat=n_lat, n_exp=n_exp)
    r = xg.shape[2]
    y = _ffn(xg.reshape(n_exp, bsz * r, d), w_gate_up, w_down, layer=layer, fc=256,
             rc=_pick(bsz * r, (768, 1024, 512, 256, 128, 64, 32, 16)))
    return _combine(y.reshape(n_exp, bsz, r, d), slot, gate, slot_t, gate_t, table, h, mod, final_g,
                    n_lat=n_lat, n_out=t, final_norm=final_norm)


def kernel(x, c, ctx, c_ctx, ada_w, ada_b, norm1_g, norm2_g, final_g, attn_w_qkv, attn_lq1, attn_lk1, attn_lq2, attn_lk2, attn_subln_g, attn_w_o, lru_w_in, lru_b_in, lru_conv_w, lru_conv_b, lru_w_gates, lru_b_gates, lru_lambda, lru_w_out, moe_w_router, moe_w_gate_up, moe_w_down):
    bsz, n_lat, d = x.shape
    n_ctx = ctx.shape[1]
    depth = ada_w.shape[0]
    assert bsz < MOD_ROWS and d % (2 * HEAD_W) == 0 and n_lat % GRID_W == 0

    cc = jnp.concatenate([c, c_ctx[None, :], jnp.zeros((MOD_ROWS - bsz - 1, d), F32)], axis=0)
    mods = _mod_tables(cc, ada_w, ada_b)
    h = jnp.concatenate([x, ctx], axis=1)
    row = lambda v: v.reshape(1, -1)

    for i in range(depth):
        last = i == depth - 1
        mod = mods[i]
        j = i // N_MIXERS
        n_exp = moe_w_router.shape[-1]
        wr = jnp.pad(moe_w_router[i], ((0, 0), (0, LANES - n_exp)))
        if i % N_MIXERS == 0:
            lam_init = 0.8 - 0.6 * math.exp(-0.3 * i)
            w = attn_w_qkv[j]
            wqk = w[:, :2 * d].astype(BF16)
            wvt = w[:, 2 * d:].T.astype(BF16)
            cos, s1, s2 = _rope_tables(n_lat, n_ctx)
            t_all = n_lat + n_ctx
            qk, vt = _qkv(h, row(norm1_g[i]), mod, wqk, wvt, cos, s1, s2, n_lat=n_lat,
                          tt=_pick(t_all, (768, 256)))
            lamv = jnp.stack([attn_lq1[j], attn_lk1[j], attn_lq2[j], attn_lk2[j]], axis=0)
            sg = attn_subln_g[j].reshape(HEAD_W, 1)
            n_rows = n_lat if last else n_lat + n_ctx
            o = _attention_lat(qk, vt, lamv, sg, n_lat=n_lat, tq=256, nh=2, lam_init=lam_init)
            if not last:
                o_ctx = _attention_ctx(qk, vt, lamv, sg, n_lat=n_lat, nh=2, lam_init=lam_init)
                o = jnp.concatenate([o, o_ctx], axis=1)
            h = _proj_res(o, attn_w_o[j].astype(BF16), h, mod, gate_chunk=2, n_rows=n_rows,
                          n_lat=n_lat, tt=_pick(n_rows, (768, 512, 256)))
        else:
            gy, xr = _lru_in(h, row(norm1_g[i]), mod, lru_w_in[j].astype(BF16),
                             row(lru_b_in[j]), n_lat=n_lat, tt=_pick(n_lat + n_ctx, (768, 256)))
            nblk = d // LRU_BW
            u = _lru_core(xr, gy, lru_conv_w[j], row(lru_conv_b[j]),
                          lru_w_gates[j].astype(BF16),
                          lru_b_gates[j].reshape(2, nblk, 1, 2 * LRU_BW), lru_lambda[j],
                          n_lat=n_lat)
            n_rows = n_lat if last else n_lat + n_ctx
            assert last, "context output of the recurrent mixer is only needed in non-final layers"
            h = _proj_res(u, lru_w_out[j].astype(BF16), h, mod, gate_chunk=2, n_rows=n_rows,
                          n_lat=n_lat, tt=_pick(n_rows, (1024, 512, 256)))
        m, p = _norm_probs(h, row(norm2_g[i]), mod, wr, n_lat=n_lat, n_exp=n_exp,
                           tt=_pick(h.shape[1], (768, 512, 256)))
        h = _moe(h, m, p, mod, moe_w_gate_up, moe_w_down, row(final_g), layer=i, n_lat=n_lat,
                 n_exp=n_exp, final_norm=last)
    return h
```
